```python
import functools
import jax, jax.numpy as jnp
from jax import lax
import numpy as np

D_MODEL = 2048
BATCH = 2
SEQ = 4096
DEPTH = 1
DEC_BATCH = 32
DEC_SEQ = 1
PAST_LEN = 16384
PAGE_SIZE = 128

GLA_HEADS = 4
GLA_DK = D_MODEL // 2 // GLA_HEADS
GLA_DV = D_MODEL // GLA_HEADS
GLA_GATE_RANK = 16
GLA_TAU = 16.0
GLA_CHUNK = 64
N_HEADS = 16
N_KV_HEADS = 4
HEAD_DIM = D_MODEL // N_HEADS
ROT_DIM = HEAD_DIM // 4
ROPE_THETA = 500000.0
IDX_HEADS = 16
IDX_DIM = 64
IDX_ROT_DIM = IDX_DIM // 4
TOPK_MAX = 256
Q_BLOCK = 128
N_EXPERTS = 64
N_GROUPS = 8
TOPK_GROUPS = 4
TOP_K = 8
D_EXPERT = 512
ROUTED_SCALE = 2.5
MOE_MAX_BLOCK = 128
NORM_EPS = 1e-6

GLA_QK_W = GLA_HEADS * GLA_DK
GLA_V_W = GLA_HEADS * GLA_DV
ATT_Q_W = N_HEADS * HEAD_DIM
ATT_KV_W = N_KV_HEADS * HEAD_DIM
IDX_Q_W = IDX_HEADS * IDX_DIM
IN_SPLITS = (GLA_QK_W, GLA_QK_W, GLA_V_W, GLA_GATE_RANK, GLA_V_W,
             ATT_Q_W, ATT_KV_W, ATT_KV_W, IDX_Q_W, IDX_DIM, IDX_HEADS,
             D_MODEL, D_MODEL)
IN_W = 2 * GLA_QK_W + 2 * GLA_V_W + GLA_GATE_RANK + ATT_Q_W + 2 * ATT_KV_W + IDX_Q_W + IDX_DIM + IDX_HEADS + 2 * D_MODEL

kernel_name = 'gated_gla_dsa_moe_decoder_step'


def rms_norm(x, g):
    xf = x.astype(jnp.float32)
    y = xf * lax.rsqrt(jnp.mean(xf * xf, axis=-1, keepdims=True) + NORM_EPS)
    return (y * g.astype(jnp.float32)).astype(x.dtype)


def rope_partial(x, pos, rot_dim):
    half = rot_dim // 2
    inv_freq = ROPE_THETA ** (-jnp.arange(half, dtype=jnp.float32) * 2.0 / rot_dim)
    ang = pos.astype(jnp.float32)[:, None] * inv_freq[None, :]
    cos = jnp.cos(ang)[None, :, None, :]
    sin = jnp.sin(ang)[None, :, None, :]
    xr = x[..., :rot_dim].astype(jnp.float32)
    x1, x2 = xr[..., :half], xr[..., half:]
    rot = jnp.concatenate([x1 * cos - x2 * sin, x2 * cos + x1 * sin], axis=-1).astype(x.dtype)
    return jnp.concatenate([rot, x[..., rot_dim:]], axis=-1)


def gla_chunked(q, k, v, log_a, s0):
    B, T, H, DK = q.shape
    DV = v.shape[-1]
    C = min(GLA_CHUNK, T)
    n = -(-T // C)
    pad = n * C - T

    def to_chunks(a):
        a = jnp.pad(a.astype(jnp.float32), ((0, 0), (0, pad), (0, 0), (0, 0)))
        return a.reshape(B, n, C, H, a.shape[-1]).transpose(1, 0, 3, 2, 4)

    qc, kc, vc, lc = to_chunks(q), to_chunks(k), to_chunks(v), to_chunks(log_a)
    bc = jnp.cumsum(lc, axis=3)
    causal = jnp.tril(jnp.ones((C, C), dtype=bool))

    def step(S, inp):
        qi, ki, vi, bi = inp
        b_end = bi[:, :, -1, :]
        q_dec = qi * jnp.exp(bi)
        k_dec = ki * jnp.exp(-bi)
        att = jnp.where(causal, jnp.einsum('bhtk,bhsk->bhts', q_dec, k_dec), 0.0)
        o = jnp.einsum('bhts,bhsv->bhtv', att, vi) + jnp.einsum('bhtk,bhkv->bhtv', q_dec, S)
        k_end = ki * jnp.exp(b_end[:, :, None, :] - bi)
        S = S * jnp.exp(b_end)[..., None] + jnp.einsum('bhsk,bhsv->bhkv', k_end, vi)
        return S, o

    S, o = lax.scan(step, s0.astype(jnp.float32), (qc, kc, vc, bc))
    o = o.transpose(1, 0, 3, 2, 4).reshape(B, n * C, H, DV)[:, :T]
    return o, S


def gla_branch(q, k, v, a_lr, r, s0, w_gla_a, b_gla_a, g_gla_norm):
    B, T, _ = q.shape
    q = q.reshape(B, T, GLA_HEADS, GLA_DK) * (GLA_DK ** -0.5)
    k = k.reshape(B, T, GLA_HEADS, GLA_DK)
    v = v.reshape(B, T, GLA_HEADS, GLA_DV)
    log_a = jax.nn.log_sigmoid((a_lr @ w_gla_a + b_gla_a).astype(jnp.float32)) / GLA_TAU
    log_a = log_a.reshape(B, T, GLA_HEADS, GLA_DK)
    o, S = gla_chunked(q, k, v, log_a, s0)
    o = rms_norm(o, g_gla_norm).reshape(B, T, GLA_V_W).astype(r.dtype)
    return o * jax.nn.silu(r), S


def indexer_topk(qi, wi, ki, q_pos, topk):
    s = jnp.einsum('bqhd,bld->bhql', qi.astype(jnp.float32), ki.astype(jnp.float32)) * (IDX_DIM ** -0.5)
    score = jnp.einsum('bhql,bqh->bql', jax.nn.relu(s), wi.astype(jnp.float32))
    L = ki.shape[1]
    admissible = jnp.arange(L)[None, None, :] <= q_pos[None, :, None]
    score = jnp.where(admissible, score, -jnp.inf)
    _, idx = lax.top_k(score, topk)
    valid = idx <= q_pos[None, :, None]
    return idx, valid


def sparse_attend(q, k_sel, v_sel, valid):
    B, Q, H, D = q.shape
    qg = q.reshape(B, Q, N_KV_HEADS, H // N_KV_HEADS, D).astype(jnp.float32)
    s = jnp.einsum('bqhgd,bqnhd->bqhgn', qg, k_sel.astype(jnp.float32)) * (D ** -0.5)
    s = jnp.where(valid[:, :, None, None, :], s, -jnp.inf)
    p = jax.nn.softmax(s, axis=-1)
    o = jnp.einsum('bqhgn,bqnhd->bqhgd', p, v_sel.astype(jnp.float32))
    return o.reshape(B, Q, H * D).astype(q.dtype)


def _gather_rows(rows, idx):
    return jax.vmap(lambda r, i: r[i])(rows, idx)


def dsa_prompt(q, k, v, qi, ki, wi):
    B, S = q.shape[0], q.shape[1]
    topk = min(TOPK_MAX, S // 4)
    nb = S // Q_BLOCK

    def blocks(a):
        return a.reshape((B, nb, Q_BLOCK) + a.shape[2:]).swapaxes(0, 1)

    q_pos = jnp.arange(S, dtype=jnp.int32).reshape(nb, Q_BLOCK)

    def one_block(inp):
        qb, qib, wib, qpos = inp
        idx, valid = indexer_topk(qib, wib, ki, qpos, topk)
        return sparse_attend(qb, _gather_rows(k, idx), _gather_rows(v, idx), valid)

    out = lax.map(one_block, (blocks(q), blocks(qi), blocks(wi), q_pos))
    return out.swapaxes(0, 1).reshape(B, S, -1)


def gather_paged(pool, new_rows, page_table, idx, past):
    DB, T, N = idx.shape
    flat = pool.reshape((-1,) + pool.shape[2:])
    is_past = idx < past
    pidx = jnp.minimum(idx, past - 1)
    page = jnp.take_along_axis(page_table, (pidx // PAGE_SIZE).reshape(DB, T * N), axis=1).reshape(DB, T, N)
    past_rows = flat[page * PAGE_SIZE + pidx % PAGE_SIZE]
    new_sel = _gather_rows(new_rows, jnp.clip(idx - past, 0, T - 1))
    mask = is_past.reshape(is_past.shape + (1,) * (past_rows.ndim - 3))
    return jnp.where(mask, past_rows, new_sel)


def dsa_sample(q, k_new, v_new, qi, ki_new, wi, cache_k, cache_v, cache_kidx, page_table):
    DB, T = q.shape[0], q.shape[1]
    past = page_table.shape[1] * PAGE_SIZE
    topk = min(TOPK_MAX, (past + T) // 4)
    ki_past = cache_kidx[page_table].reshape(DB, past, IDX_DIM)
    ki_all = jnp.concatenate([ki_past.astype(ki_new.dtype), ki_new], axis=1)
    q_pos = past + jnp.arange(T, dtype=jnp.int32)
    idx, valid = indexer_topk(qi, wi, ki_all, q_pos, topk)
    k_sel = gather_paged(cache_k.astype(k_new.dtype), k_new, page_table, idx, past)
    v_sel = gather_paged(cache_v.astype(v_new.dtype), v_new, page_table, idx, past)
    return sparse_attend(q, k_sel, v_sel, valid)


def swiglu(h, w_gate, w_up, w_down):
    return (jax.nn.silu(h @ w_gate) * (h @ w_up)) @ w_down


def routed_experts(h, eidx, wts, w_exp_gate, w_exp_up, w_exp_down):
    T, D = h.shape
    K = eidx.shape[1]
    E = w_exp_gate.shape[0]
    M = T * K
    bm = 8
    while bm < MOE_MAX_BLOCK and 2 * bm * E <= M:
        bm *= 2
    n_blocks = (M + E * (bm - 1) + bm - 1) // bm
    P = n_blocks * bm
    e_flat = eidx.reshape(M)
    tok = jnp.repeat(jnp.arange(T, dtype=jnp.int32), K)
    w_flat = wts.reshape(M)
    order = jnp.argsort(e_flat, stable=True)
    e_sorted = e_flat[order]
    counts = jnp.bincount(e_flat, length=E)
    starts = jnp.cumsum(counts) - counts
    padded = (counts + bm - 1) // bm * bm
    pends = jnp.cumsum(padded)
    pstarts = pends - padded
    dest = pstarts[e_sorted] + (jnp.arange(M) - starts[e_sorted])
    row_tok = jnp.full((P,), T, jnp.int32).at[dest].set(tok[order])
    row_w = jnp.zeros((P,), jnp.float32).at[dest].set(w_flat[order])
    block_e = jnp.clip(jnp.searchsorted(pends, jnp.arange(n_blocks) * bm, side='right'), 0, E - 1)
    h_pad = jnp.concatenate([h, jnp.zeros((1, D), h.dtype)], axis=0)
    xs = h_pad[row_tok].reshape(n_blocks, bm, D)

    def expert_block(inp):
        xb, e = inp
        return swiglu(xb, w_exp_gate[e], w_exp_up[e], w_exp_down[e])

    ys = lax.map(expert_block, (xs, block_e)).reshape(P, D)
    out = jax.ops.segment_sum(ys * row_w[:, None].astype(ys.dtype), row_tok, num_segments=T + 1)
    return out[:T]


def moe_ffn(h, w_router, b_router, w_sh_gate, w_sh_up, w_sh_down, w_exp_gate, w_exp_up, w_exp_down):
    T = h.shape[0]
    s = jax.nn.sigmoid(h.astype(jnp.float32) @ w_router.astype(jnp.float32))
    s_sel = s + b_router.astype(jnp.float32)
    per_group = N_EXPERTS // N_GROUPS
    grp_score = lax.top_k(s_sel.reshape(T, N_GROUPS, per_group), 2)[0].sum(-1)
    _, grp_idx = lax.top_k(grp_score, TOPK_GROUPS)
    grp_mask = jnp.any(grp_idx[:, :, None] == jnp.arange(N_GROUPS)[None, None, :], axis=1)
    s_masked = jnp.where(jnp.repeat(grp_mask, per_group, axis=-1), s_sel, -jnp.inf)
    _, eidx = lax.top_k(s_masked, TOP_K)
    wts = jnp.take_along_axis(s, eidx, axis=-1)
    wts = wts / jnp.sum(wts, axis=-1, keepdims=True) * ROUTED_SCALE
    routed = routed_experts(h, eidx, wts, w_exp_gate, w_exp_up, w_exp_down)
    return swiglu(h, w_sh_gate, w_sh_up, w_sh_down) + routed


def decoder_layer(x, c, pos, gla_s0, attend, w_ada, b_ada, g_norm1, w_in, w_gla_a, b_gla_a, g_gla_norm,
                  w_gla_o, w_attn_o, w_out, g_norm2, w_router, b_router, w_sh_gate, w_sh_up, w_sh_down,
                  w_exp_gate, w_exp_up, w_exp_down):
    B, T, D = x.shape
    mod = jnp.dot(jax.nn.silu(c), w_ada) + b_ada
    sh1, sc1, gt1, sh2, sc2, gt2 = [m[:, None, :] for m in jnp.split(mod, 6, axis=-1)]
    h = rms_norm(x, g_norm1) * (1.0 + sc1) + sh1
    pts = np.cumsum(IN_SPLITS)[:-1].tolist()
    gq, gk, gv, ga, gr, aq, ak, av, iq, ik, iw, za, zb = jnp.split(h @ w_in, pts, axis=-1)
    o_gla, s_gla = gla_branch(gq, gk, gv, ga, gr, gla_s0, w_gla_a, b_gla_a, g_gla_norm)
    aq = rope_partial(aq.reshape(B, T, N_HEADS, HEAD_DIM), pos, ROT_DIM)
    ak = rope_partial(ak.reshape(B, T, N_KV_HEADS, HEAD_DIM), pos, ROT_DIM)
    av = av.reshape(B, T, N_KV_HEADS, HEAD_DIM)
    iq = rope_partial(iq.reshape(B, T, IDX_HEADS, IDX_DIM), pos, IDX_ROT_DIM)
    ik = rope_partial(ik.reshape(B, T, 1, IDX_DIM), pos, IDX_ROT_DIM)[:, :, 0, :]
    iw = iw * (IDX_HEADS ** -0.5)
    o_att = attend(aq, ak, av, iq, ik, iw)
    merged = jax.nn.sigmoid(za) * (o_gla @ w_gla_o) + jax.nn.sigmoid(zb) * (o_att @ w_attn_o)
    x = x + gt1 * (merged @ w_out)
    h2 = rms_norm(x, g_norm2) * (1.0 + sc2) + sh2
    ffn = moe_ffn(h2.reshape(B * T, D), w_router, b_router, w_sh_gate, w_sh_up, w_sh_down,
                  w_exp_gate, w_exp_up, w_exp_down)
    x = x + gt2 * ffn.reshape(B, T, D)
    return x, (ak, av, ik, s_gla)


def setup_inputs(seed: int = 0) -> dict:
    key = jax.random.key(seed)
    keys = iter(jax.random.split(key, 48))

    def nrm(shape, scale):
        return jax.random.normal(next(keys), shape, jnp.float32) * scale

    n_pages = PAST_LEN // PAGE_SIZE
    n_used = DEC_BATCH * n_pages
    n_pool = n_used + max(1, n_used // 4)
    L = DEPTH
    page_table = jax.random.permutation(next(keys), n_pool)[:n_used].reshape(DEC_BATCH, n_pages).astype(jnp.int32)
    return {
        'x_prompt': nrm((BATCH, SEQ, D_MODEL), 1.0),
        'x_sample': nrm((DEC_BATCH, DEC_SEQ, D_MODEL), 1.0),
        'c_prompt': nrm((BATCH, D_MODEL), 1.0),
        'c_sample': nrm((DEC_BATCH, D_MODEL), 1.0),
        'cache_k': nrm((L, n_pool, PAGE_SIZE, N_KV_HEADS, HEAD_DIM), 1.0),
        'cache_v': nrm((L, n_pool, PAGE_SIZE, N_KV_HEADS, HEAD_DIM), 1.0),
        'cache_kidx': nrm((L, n_pool, PAGE_SIZE, IDX_DIM), 1.0),
        'state_gla': nrm((L, DEC_BATCH, GLA_HEADS, GLA_DK, GLA_DV), 0.5),
        'page_table': page_table,
        'w_ada': nrm((L, D_MODEL, 6 * D_MODEL), 0.5 * D_MODEL ** -0.5),
        'b_ada': nrm((L, 6 * D_MODEL), 0.02),
        'g_norm1': 1.0 + nrm((L, D_MODEL), 0.02),
        'w_in': nrm((L, D_MODEL, IN_W), D_MODEL ** -0.5),
        'w_gla_a': nrm((L, GLA_GATE_RANK, GLA_QK_W), GLA_GATE_RANK ** -0.5),
        'b_gla_a': nrm((L, GLA_QK_W), 0.1),
        'g_gla_norm': 1.0 + nrm((L, GLA_DV), 0.02),
        'w_gla_o': nrm((L, GLA_V_W, D_MODEL), GLA_V_W ** -0.5),
        'w_attn_o': nrm((L, ATT_Q_W, D_MODEL), ATT_Q_W ** -0.5),
        'w_out': nrm((L, D_MODEL, D_MODEL), D_MODEL ** -0.5),
        'g_norm2': 1.0 + nrm((L, D_MODEL), 0.02),
        'w_router': nrm((L, D_MODEL, N_EXPERTS), D_MODEL ** -0.5),
        'b_router': nrm((L, N_EXPERTS), 0.01),
        'w_sh_gate': nrm((L, D_MODEL, D_EXPERT), D_MODEL ** -0.5),
        'w_sh_up': nrm((L, D_MODEL, D_EXPERT), D_MODEL ** -0.5),
        'w_sh_down': nrm((L, D_EXPERT, D_MODEL), D_EXPERT ** -0.5),
        'w_exp_gate': nrm((L, N_EXPERTS, D_MODEL, D_EXPERT), D_MODEL ** -0.5),
        'w_exp_up': nrm((L, N_EXPERTS, D_MODEL, D_EXPERT), D_MODEL ** -0.5),
        'w_exp_down': nrm((L, N_EXPERTS, D_EXPERT, D_MODEL), D_EXPERT ** -0.5),
        'g_final': 1.0 + nrm((D_MODEL,), 0.02),
    }


def reference(x_prompt, x_sample, c_prompt, c_sample, cache_k, cache_v, cache_kidx, state_gla, page_table,
              w_ada, b_ada, g_norm1, w_in, w_gla_a, b_gla_a, g_gla_norm, w_gla_o, w_attn_o, w_out, g_norm2,
              w_router, b_router, w_sh_gate, w_sh_up, w_sh_down, w_exp_gate, w_exp_up, w_exp_down, g_final):
    Bp, Tp = x_prompt.shape[0], x_prompt.shape[1]
    Ts = x_sample.shape[1]
    pos_p = jnp.arange(Tp, dtype=jnp.int32)
    pos_s = PAST_LEN + jnp.arange(Ts, dtype=jnp.int32)
    yp, ys = x_prompt, x_sample
    new_p, new_s = [], []
    for l in range(DEPTH):
        lw = (w_ada[l], b_ada[l], g_norm1[l], w_in[l], w_gla_a[l], b_gla_a[l], g_gla_norm[l], w_gla_o[l],
              w_attn_o[l], w_out[l], g_norm2[l], w_router[l], b_router[l], w_sh_gate[l], w_sh_up[l],
              w_sh_down[l], w_exp_gate[l], w_exp_up[l], w_exp_down[l])
        s0_p = jnp.zeros((Bp, GLA_HEADS, GLA_DK, GLA_DV), jnp.float32)
        yp, st_p = decoder_layer(yp, c_prompt, pos_p, s0_p, dsa_prompt, *lw)
        attend_s = functools.partial(dsa_sample, cache_k=cache_k[l], cache_v=cache_v[l],
                                     cache_kidx=cache_kidx[l], page_table=page_table)
        ys, st_s = decoder_layer(ys, c_sample, pos_s, state_gla[l], attend_s, *lw)
        new_p.append(st_p)
        new_s.append(st_s)
    y_prompt = rms_norm(yp, g_final)
    y_sample = rms_norm(ys, g_final)
    k_prompt = jnp.stack([st[0] for st in new_p])
    v_prompt = jnp.stack([st[1] for st in new_p])
    kidx_prompt = jnp.stack([st[2] for st in new_p])
    gla_prompt = jnp.stack([st[3] for st in new_p])
    k_sample = jnp.stack([st[0] for st in new_s])
    v_sample = jnp.stack([st[1] for st in new_s])
    kidx_sample = jnp.stack([st[2] for st in new_s])
    gla_sample = jnp.stack([st[3] for st in new_s])
    return (y_prompt, y_sample, k_prompt, v_prompt, kidx_prompt, gla_prompt, k_sample, v_sample, kidx_sample, gla_sample)
```

```python
import functools
import math
from typing import NamedTuple

import jax
import jax.numpy as jnp
import numpy as np
from jax import lax
from jax.experimental import pallas as pl
from jax.experimental.pallas import tpu as pltpu

F32 = jnp.float32
BF16 = jnp.bfloat16
I32 = jnp.int32

V7X_LANES = 128
V7X_SUBLANES = 8
V7X_VMEM_LIMIT_BYTES = 56 * 1024 * 1024

GLA_TAU = 16.0
GLA_CHUNK = 64
ROPE_THETA = 500000.0
TOPK_MAX = 256
Q_BLOCK = 128
KEY_CHUNK = 512
N_GROUPS = 8
TOPK_GROUPS = 4
TOP_K = 8
ROUTED_SCALE = 2.5
NORM_EPS = 1e-6
INT_MIN = -(2 ** 31)
INT_MAX = 2 ** 31 - 1
NEG_BIG = -1e30


class Cfg(NamedTuple):
    D: int
    B: int
    T: int
    DB: int
    GH: int
    DK: int
    DV: int
    RANK: int
    NH: int
    NKV: int
    HD: int
    IH: int
    ID: int
    E: int
    DE: int
    PAGE: int
    NPAGES: int
    PAST: int


def _cparams(sem, vmem=V7X_VMEM_LIMIT_BYTES):
    return pltpu.CompilerParams(dimension_semantics=sem, vmem_limit_bytes=vmem)


def _bdot(a, b):
    return jnp.dot(a.astype(BF16), b.astype(BF16), preferred_element_type=F32)


def _dot_nt(a, b):
    return lax.dot_general(a.astype(BF16), b.astype(BF16), (((1,), (1,)), ((), ())),
                           preferred_element_type=F32)


def _dot_tn(a, b):
    return lax.dot_general(a.astype(BF16), b.astype(BF16), (((0,), (0,)), ((), ())),
                           preferred_element_type=F32)


def _split2(x):
    hi = x.astype(BF16)
    lo = (x - hi.astype(F32)).astype(BF16)
    return hi, lo


def _split3(x):
    hi = x.astype(BF16)
    r1 = x - hi.astype(F32)
    mid = r1.astype(BF16)
    lo = (r1 - mid.astype(F32)).astype(BF16)
    return hi, mid, lo


def _sigmoid(x):
    return 1.0 / (1.0 + jnp.exp(-x))


def _silu(x):
    return x * _sigmoid(x)


def _ada_kernel(c_ref, w_ref, b_ref, o_ref):
    c = c_ref[...]
    o_ref[...] = _bdot(_silu(c), w_ref[...]) + b_ref[...]


def ada_mod(c, w_ada, b_ada):
    R, D = c.shape
    N = w_ada.shape[1]
    tn = 512 if N % 512 == 0 else 128
    return pl.pallas_call(
        _ada_kernel,
        grid=(N // tn,),
        in_specs=[pl.BlockSpec((R, D), lambda j: (0, 0)),
                  pl.BlockSpec((D, tn), lambda j: (0, j)),
                  pl.BlockSpec((1, tn), lambda j: (0, j))],
        out_specs=pl.BlockSpec((R, tn), lambda j: (0, j)),
        out_shape=jax.ShapeDtypeStruct((R, N), F32),
        compiler_params=_cparams(("arbitrary",)),
        name="ada_mod",
    )(c, w_ada, b_ada.reshape(1, N))


class ProjLayout(NamedTuple):
    names: tuple
    offs: dict
    widths: dict
    total: int
    tn: int
    chunk_types: tuple


def proj_layout(cfg: Cfg) -> ProjLayout:
    qk = cfg.GH * cfg.DK
    vv = cfg.GH * cfg.DV
    aq = cfg.NH * cfg.HD
    kv = cfg.NKV * cfg.HD
    iq = cfg.IH * cfg.ID
    big = [("gq", qk), ("gk", qk), ("gv", vv), ("gr", vv), ("aq", aq), ("ak", kv), ("av", kv),
           ("iq", iq), ("za", cfg.D), ("zb", cfg.D)]
    tn = 512
    for _, w in big:
        tn = math.gcd(tn, w)
    assert tn % V7X_LANES == 0
    offs, widths, types = {}, {}, []
    off = 0
    for n, w in big:
        offs[n], widths[n] = off, w
        t = 1 if n in ("aq", "ak") else (2 if n == "iq" else 0)
        types += [t] * (w // V7X_LANES)
        off += w
    for n, t in (("ga", 0), ("ik", 2), ("iw", 0), ("pad", 0)):
        offs[n], widths[n] = off, V7X_LANES
        types.append(t)
        off += V7X_LANES
    assert off % tn == 0
    return ProjLayout(tuple(n for n, _ in big) + ("ga", "ik", "iw", "pad"), offs, widths, off, tn, tuple(types))


def relayout_w_in(w_in, cfg: Cfg, lay: ProjLayout):
    qk = cfg.GH * cfg.DK
    vv = cfg.GH * cfg.DV
    aq = cfg.NH * cfg.HD
    kv = cfg.NKV * cfg.HD
    iq = cfg.IH * cfg.ID
    src_w = dict(gq=qk, gk=qk, gv=vv, ga=cfg.RANK, gr=vv, aq=aq, ak=kv, av=kv, iq=iq, ik=cfg.ID, iw=cfg.IH,
                 za=cfg.D, zb=cfg.D)
    src_off, o = {}, 0
    for n in ("gq", "gk", "gv", "ga", "gr", "aq", "ak", "av", "iq", "ik", "iw", "za", "zb"):
        src_off[n] = o
        o += src_w[n]
    assert o == w_in.shape[1]
    D = w_in.shape[0]

    def cols(n):
        return w_in[:, src_off[n]:src_off[n] + src_w[n]]

    def padded(a):
        return jnp.pad(a, ((0, 0), (0, V7X_LANES - a.shape[1])))

    assert 2 * cfg.ID == V7X_LANES and cfg.RANK <= V7X_LANES and cfg.IH <= V7X_LANES
    parts = [cols(n) for n in ("gq", "gk", "gv", "gr", "aq", "ak", "av", "iq", "za", "zb")]
    parts += [padded(cols("ga")), cols("ik"), cols("ik"), padded(cols("iw")), jnp.zeros((D, V7X_LANES), w_in.dtype)]
    return jnp.concatenate(parts, axis=1).astype(BF16)


def rope_tables(pos, cfg: Cfg):
    pos = np.asarray(pos, np.float64)

    def one(width, rot):
        half = rot // 2
        inv = ROPE_THETA ** (-np.arange(half, dtype=np.float64) * 2.0 / rot)
        ang = pos[:, None] * inv[None, :]
        lane = np.arange(V7X_LANES) % width
        c = np.where(lane < rot, np.cos(ang)[:, lane % half], 1.0)
        s1 = np.where((lane >= half) & (lane < rot), np.sin(ang)[:, (lane - half) % half], 0.0)
        s2 = np.where(lane < half, -np.sin(ang)[:, lane % half], 0.0)
        return [c, s1, s2]

    tabs = one(cfg.HD, cfg.HD // 4) + one(cfg.ID, cfg.ID // 4)
    return jnp.asarray(np.concatenate(tabs, axis=1), F32)


def _inproj_kernel(x_ref, g_ref, sc_ref, sh_ref, w_ref, rope_ref, o_ref, h_ref, *, patterns, half_a, half_i):
    j = pl.program_id(1)

    @pl.when(j == 0)
    def _():
        x = x_ref[...]
        ms = jnp.mean(x * x, axis=-1, keepdims=True)
        y = x * lax.rsqrt(ms + NORM_EPS) * g_ref[...]
        h_ref[...] = (y * (1.0 + sc_ref[...]) + sh_ref[...]).astype(BF16)

    acc = jnp.dot(h_ref[...], w_ref[...], preferred_element_type=F32)
    L = V7X_LANES

    def rot(a, t):
        base, half = (0, half_a) if t == 1 else (3 * L, half_i)
        c = rope_ref[:, base:base + L]
        s1 = rope_ref[:, base + L:base + 2 * L]
        s2 = rope_ref[:, base + 2 * L:base + 3 * L]
        return a * c + pltpu.roll(a, half, 1) * s1 + pltpu.roll(a, L - half, 1) * s2

    for pat, ranges in patterns:
        cond = None
        for lo, hi in ranges:
            c = (j >= lo) & (j < hi)
            cond = c if cond is None else (cond | c)

        @pl.when(cond)
        def _(pat=pat):
            if all(t == 0 for t in pat):
                o_ref[...] = acc
            else:
                for ci, t in enumerate(pat):
                    a = acc[:, ci * L:(ci + 1) * L]
                    o_ref[:, ci * L:(ci + 1) * L] = a if t == 0 else rot(a, t)


def _tile_patterns(lay: ProjLayout):
    per = lay.tn // V7X_LANES
    ntiles = lay.total // lay.tn
    pats = {}
    for j in range(ntiles):
        pats.setdefault(tuple(lay.chunk_types[j * per:(j + 1) * per]), []).append(j)
    out = []
    for pat, js in pats.items():
        ranges, start, prev = [], js[0], js[0]
        for j in js[1:]:
            if j != prev + 1:
                ranges.append((start, prev + 1))
                start = j
            prev = j
        ranges.append((start, prev + 1))
        out.append((pat, tuple(ranges)))
    return tuple(out)


def in_proj(x, g, sc, sh, w_pad, rope, cfg: Cfg, lay: ProjLayout, tm, rows_per_group):
    N, D = x.shape
    G, R, _ = sc.shape
    tiles_per_group = rows_per_group // tm
    assert N == G * rows_per_group and rows_per_group % tm == 0 and R in (1, tm)
    assert rope.shape[0] == rows_per_group
    kern = functools.partial(_inproj_kernel, patterns=_tile_patterns(lay), half_a=cfg.HD // 8, half_i=cfg.ID // 8)
    mod_spec = pl.BlockSpec((None, R, D), lambda i, j: (i // tiles_per_group, 0, 0))
    return pl.pallas_call(
        kern,
        grid=(N // tm, lay.total // lay.tn),
        in_specs=[pl.BlockSpec((tm, D), lambda i, j: (i, 0)),
                  pl.BlockSpec((1, D), lambda i, j: (0, 0)),
                  mod_spec, mod_spec,
                  pl.BlockSpec((D, lay.tn), lambda i, j: (0, j)),
                  pl.BlockSpec((tm, 6 * V7X_LANES), lambda i, j: (i % tiles_per_group, 0))],
        out_specs=pl.BlockSpec((tm, lay.tn), lambda i, j: (i, j)),
        out_shape=jax.ShapeDtypeStruct((N, lay.total), F32),
        scratch_shapes=[pltpu.VMEM((tm, D), BF16)],
        compiler_params=_cparams(("arbitrary", "arbitrary")),
        name="in_proj",
    )(x, g.reshape(1, D), sc, sh, w_pad, rope)


def _log_sigmoid(z):
    return jnp.minimum(z, 0.0) - jnp.log1p(jnp.exp(-jnp.abs(z)))


def _gla_prompt_kernel(q_ref, k_ref, v_ref, r_ref, a_ref, wa_ref, ba_ref, gn_ref, o_ref, s_ref, st_ref,
                       *, C, nchunk, rank, scale):
    t = pl.program_id(2)

    @pl.when(t == 0)
    def _():
        st_ref[...] = jnp.zeros_like(st_ref)

    row = lax.broadcasted_iota(I32, (C, C), 0)
    col = lax.broadcasted_iota(I32, (C, C), 1)
    causal = row >= col
    tri = jnp.where(causal, 1.0, 0.0).astype(BF16)
    wa = wa_ref[...]
    ba = ba_ref[...]
    gn = gn_ref[...]
    for c in range(nchunk):
        sl = slice(c * C, (c + 1) * C)
        q = q_ref[0, sl, :]
        k = k_ref[0, sl, :]
        v = v_ref[0, sl, :]
        a = a_ref[0, sl, :][:, :rank]
        la = _log_sigmoid(_bdot(a, wa) + ba) * (1.0 / GLA_TAU)
        hi, mid, lo = _split3(la)
        bc = (jnp.dot(tri, hi, preferred_element_type=F32) + jnp.dot(tri, mid, preferred_element_type=F32)
              + jnp.dot(tri, lo, preferred_element_type=F32))
        bend = bc[C - 1:C, :]
        qd = q * scale * jnp.exp(bc)
        kd = k * jnp.exp(-bc)
        ke = k * jnp.exp(bend - bc)
        att = jnp.where(causal, _dot_nt(qd, kd), 0.0)
        st = st_ref[...]
        o = _bdot(att, v) + _dot_nt(qd, st)
        st_ref[...] = st * jnp.exp(bend) + _dot_tn(v, ke)
        ms = jnp.mean(o * o, axis=-1, keepdims=True)
        on = o * lax.rsqrt(ms + NORM_EPS) * gn
        o_ref[0, sl, :] = on * _silu(r_ref[0, sl, :])

    @pl.when(t == pl.num_programs(2) - 1)
    def _():
        s_ref[0, 0] = st_ref[...].T


def gla_prompt(proj, w_gla_a, b_gla_a, g_gla_norm, cfg: Cfg, lay: ProjLayout):
    B, T, _ = proj.shape
    C = min(GLA_CHUNK, T)
    assert T % C == 0
    tc = C
    while tc * 2 <= 512 and T % (tc * 2) == 0:
        tc *= 2
    DK, DV, GH = cfg.DK, cfg.DV, cfg.GH
    oq, ok, ov, orr, oa = (lay.offs[n] for n in ("gq", "gk", "gv", "gr", "ga"))
    assert oq % DK == 0 and ok % DK == 0 and ov % DV == 0 and orr % DV == 0 and oa % V7X_LANES == 0
    kern = functools.partial(_gla_prompt_kernel, C=C, nchunk=tc // C, rank=cfg.RANK, scale=DK ** -0.5)
    return pl.pallas_call(
        kern,
        grid=(B, GH, T // tc),
        in_specs=[pl.BlockSpec((1, tc, DK), lambda b, h, t: (b, t, oq // DK + h)),
                  pl.BlockSpec((1, tc, DK), lambda b, h, t: (b, t, ok // DK + h)),
                  pl.BlockSpec((1, tc, DV), lambda b, h, t: (b, t, ov // DV + h)),
                  pl.BlockSpec((1, tc, DV), lambda b, h, t: (b, t, orr // DV + h)),
                  pl.BlockSpec((1, tc, V7X_LANES), lambda b, h, t: (b, t, oa // V7X_LANES)),
                  pl.BlockSpec((cfg.RANK, DK), lambda b, h, t: (0, h)),
                  pl.BlockSpec((1, DK), lambda b, h, t: (0, h)),
                  pl.BlockSpec((1, DV), lambda b, h, t: (0, 0))],
        out_specs=[pl.BlockSpec((1, tc, DV), lambda b, h, t: (b, t, h)),
                   pl.BlockSpec((1, 1, DK, DV), lambda b, h, t: (b, h, 0, 0))],
        out_shape=[jax.ShapeDtypeStruct((B, T, GH * DV), F32),
                   jax.ShapeDtypeStruct((B, GH, DK, DV), F32)],
        scratch_shapes=[pltpu.VMEM((DV, DK), F32)],
        compiler_params=_cparams(("arbitrary", "arbitrary", "arbitrary")),
        name="gla_prompt",
    )(proj, proj, proj, proj, proj, w_gla_a, b_gla_a.reshape(1, -1), g_gla_norm.reshape(1, DV))


def _gla_sample_kernel(q_ref, k_ref, v_ref, r_ref, a_ref, wa_ref, ba_ref, gn_ref, s0_ref, o_ref, s_ref,
                       *, GH, DK, DV, rank, scale):
    a = jnp.broadcast_to(a_ref[0][:, :rank], (V7X_SUBLANES, rank))
    ones = jnp.ones((V7X_SUBLANES, V7X_LANES), BF16)
    first = lax.broadcasted_iota(I32, (V7X_SUBLANES, 1), 0) == 0
    for h in range(GH):
        q = q_ref[0][:, h * DK:(h + 1) * DK]
        k = k_ref[0][:, h * DK:(h + 1) * DK]
        v = v_ref[0][:, h * DV:(h + 1) * DV]
        la = _log_sigmoid(_bdot(a, wa_ref[:, h * DK:(h + 1) * DK]) + ba_ref[:, h * DK:(h + 1) * DK])
        la = la[0:1, :] * (1.0 / GLA_TAU)
        dec = jnp.exp(la)
        qd = q * scale * dec
        kd = k * jnp.exp(-la)
        att = jnp.sum(qd * kd, axis=-1, keepdims=True)
        s0 = s0_ref[0, h]
        qd8 = jnp.broadcast_to(qd, (V7X_SUBLANES, DK))
        qh, ql = _split2(qd8)
        sh, sl = _split2(s0)
        qs = (jnp.dot(qh, sh, preferred_element_type=F32) + jnp.dot(qh, sl, preferred_element_type=F32)
              + jnp.dot(ql, sh, preferred_element_type=F32))[0:1, :]
        o = att * v + qs
        def col(x, width):
            x8 = jnp.where(first, jnp.broadcast_to(x, (V7X_SUBLANES, x.shape[1])), 0.0)
            parts = _split3(x8)
            rhs = ones if width == V7X_LANES else jnp.ones((V7X_SUBLANES, width), BF16)
            acc = None
            for p in parts:
                d = lax.dot_general(p, rhs, (((0,), (0,)), ((), ())), preferred_element_type=F32)
                acc = d if acc is None else acc + d
            return acc

        dec_col = col(dec, V7X_LANES)[:, 0:1]
        k8 = jnp.where(first, jnp.broadcast_to(k, (V7X_SUBLANES, DK)), 0.0)
        v8 = jnp.where(first, jnp.broadcast_to(v, (V7X_SUBLANES, DV)), 0.0)
        kh, kl = _split2(k8)
        vh, vl = _split2(v8)
        dn = (((0,), (0,)), ((), ()))
        outer = (lax.dot_general(kh, vh, dn, preferred_element_type=F32)
                 + lax.dot_general(kh, vl, dn, preferred_element_type=F32)
                 + lax.dot_general(kl, vh, dn, preferred_element_type=F32))
        s_ref[0, h] = s0 * dec_col + outer
        ms = jnp.mean(o * o, axis=-1, keepdims=True)
        on = o * lax.rsqrt(ms + NORM_EPS) * gn_ref[...]
        o_ref[0, :, h * DV:(h + 1) * DV] = on * _silu(r_ref[0][:, h * DV:(h + 1) * DV])


def gla_sample(proj, state, w_gla_a, b_gla_a, g_gla_norm, cfg: Cfg, lay: ProjLayout):
    DB = proj.shape[0]
    DK, DV, GH = cfg.DK, cfg.DV, cfg.GH
    qk, vv = GH * DK, GH * DV
    oq, ok, ov, orr, oa = (lay.offs[n] for n in ("gq", "gk", "gv", "gr", "ga"))
    assert oq % qk == 0 and ok % qk == 0 and ov % vv == 0 and orr % vv == 0
    kern = functools.partial(_gla_sample_kernel, GH=GH, DK=DK, DV=DV, rank=cfg.RANK, scale=DK ** -0.5)
    return pl.pallas_call(
        kern,
        grid=(DB,),
        in_specs=[pl.BlockSpec((1, 1, qk), lambda b: (b, 0, oq // qk)),
                  pl.BlockSpec((1, 1, qk), lambda b: (b, 0, ok // qk)),
                  pl.BlockSpec((1, 1, vv), lambda b: (b, 0, ov // vv)),
                  pl.BlockSpec((1, 1, vv), lambda b: (b, 0, orr // vv)),
                  pl.BlockSpec((1, 1, V7X_LANES), lambda b: (b, 0, oa // V7X_LANES)),
                  pl.BlockSpec((cfg.RANK, qk), lambda b: (0, 0)),
                  pl.BlockSpec((1, qk), lambda b: (0, 0)),
                  pl.BlockSpec((1, DV), lambda b: (0, 0)),
                  pl.BlockSpec((1, GH, DK, DV), lambda b: (b, 0, 0, 0))],
        out_specs=[pl.BlockSpec((1, 1, vv), lambda b: (b, 0, 0)),
                   pl.BlockSpec((1, GH, DK, DV), lambda b: (b, 0, 0, 0))],
        out_shape=[jax.ShapeDtypeStruct((DB, 1, vv), F32),
                   jax.ShapeDtypeStruct((DB, GH, DK, DV), F32)],
        compiler_params=_cparams(("arbitrary",)),
        name="gla_sample",
    )(proj, proj, proj, proj, proj, w_gla_a, b_gla_a.reshape(1, -1), g_gla_norm.reshape(1, DV), state)


def _offset_word_to_f32(t):
    key = t ^ INT_MIN
    return pltpu.bitcast(key ^ ((key >> 31) & INT_MAX), F32)


def _dsa_prompt_kernel(iq_ref, iw_ref, aq_ref, kidx_ref, kbf_ref, vbf_ref, o_ref,
                       sc_ref, lg_ref, qhi_ref, qlo_ref, wb_ref, *, IH, ID, NH, G, HD, topk, wscale, sm_scale):
    qb = pl.program_id(1)
    QB = Q_BLOCK
    L = V7X_LANES
    CH = KEY_CHUNK // L
    nch = qb // CH + 1
    lane = lax.broadcasted_iota(I32, (QB, L), 1)
    rowq = lax.broadcasted_iota(I32, (QB, 1), 0) + qb * QB

    for h in range(IH):
        pair = iq_ref[0][:, (h // 2) * V7X_LANES:(h // 2 + 1) * V7X_LANES]
        mine = (lane // ID) == (h % 2)
        qh = jnp.where(mine, pair, 0.0)
        hi, lo = _split2(qh)
        qhi_ref[h * QB:(h + 1) * QB, :] = hi
        qlo_ref[h * QB:(h + 1) * QB, :] = lo
        wb_ref[h * QB:(h + 1) * QB, :] = jnp.broadcast_to(iw_ref[0][:, h:h + 1] * wscale, (QB, 2 * L))

    lane2 = lax.broadcasted_iota(I32, (QB, 2 * L), 1)

    def score_body(c, carry):
        for half in range(CH // 2):
            k0 = c * KEY_CHUNK + half * 2 * L
            kblk = kidx_ref[0, pl.ds(pl.multiple_of(k0, 2 * L), 2 * L), :]
            khi, klo = _split2(kblk)
            dn = (((1,), (1,)), ((), ()))
            s = (lax.dot_general(qhi_ref[...], khi, dn, preferred_element_type=F32)
                 + lax.dot_general(qhi_ref[...], klo, dn, preferred_element_type=F32)
                 + lax.dot_general(qlo_ref[...], khi, dn, preferred_element_type=F32))
            s = jnp.maximum(s, 0.0) * wb_ref[...]
            sc = s[0:QB]
            for h in range(1, IH):
                sc = sc + s[h * QB:(h + 1) * QB]
            sc_ref[c, :, half * 2 * L:(half + 1) * 2 * L] = jnp.where(k0 + lane2 <= rowq, sc, -jnp.inf)
        return carry

    lax.fori_loop(0, nch, score_body, 0)

    def sweep(fn, init, combine):
        def body(c, acc):
            slab = sc_ref[c]
            for j in range(CH):
                acc = combine(acc, fn(slab[:, j * L:(j + 1) * L], c * KEY_CHUNK + j * L + lane))
            return acc
        return lax.fori_loop(0, nch, body, init)

    def count(pred):
        acc = sweep(lambda s, pos: jnp.where(pred(s, pos), 1, 0), jnp.zeros((QB, L), I32), jnp.add)
        return jnp.sum(acc, axis=1, keepdims=True)

    want = jnp.minimum(rowq + 1, topk)

    def bit_body(i, t):
        cand = t | (jnp.int32(1) << (31 - i))
        cf = _offset_word_to_f32(cand)
        cnt = count(lambda s, pos: s >= cf)
        return jnp.where(cnt >= want, cand, t)

    thr0 = _offset_word_to_f32(lax.fori_loop(0, 32, bit_body, jnp.zeros((QB, 1), I32)))
    n_ge = count(lambda s, pos: s >= thr0)
    thr = jnp.min(sweep(lambda s, pos: jnp.where(s >= thr0, s, jnp.inf), jnp.full((QB, L), jnp.inf, F32),
                        jnp.minimum), axis=1, keepdims=True)
    cut = jnp.full((QB, 1), INT_MAX, I32)

    def selected(s, pos, thr, cut):
        return (s > thr) | ((s == thr) & (pos <= cut))

    def drop_last(_, tc):
        thr, cut = tc
        over = count(lambda s, pos: selected(s, pos, thr, cut)) > want
        low = jnp.min(sweep(lambda s, pos: jnp.where(selected(s, pos, thr, cut), s, jnp.inf),
                            jnp.full((QB, L), jnp.inf, F32), jnp.minimum), axis=1, keepdims=True)
        last = jnp.max(sweep(lambda s, pos: jnp.where(selected(s, pos, thr, cut) & (s == low), pos, -1),
                             jnp.full((QB, L), -1, I32), jnp.maximum), axis=1, keepdims=True)
        return jnp.where(over, low, thr), jnp.where(over, last - 1, cut)

    thr, cut = lax.fori_loop(0, jnp.max(n_ge - want), drop_last, (thr, cut))

    def mask_body(c, carry):
        slab = sc_ref[c]
        pos = c * KEY_CHUNK + lax.broadcasted_iota(I32, slab.shape, 1)
        sc_ref[c] = jnp.where(selected(slab, pos, thr, cut), 0.0, -jnp.inf)
        return carry

    lax.fori_loop(0, nch, mask_body, 0)

    c2 = sm_scale * math.log2(math.e)
    for g in range(NH // G):
        qg = jnp.concatenate([aq_ref[0][:, (g * G + j) * HD:(g * G + j + 1) * HD] for j in range(G)],
                             axis=0).astype(BF16)

        def logit_body(c, mx):
            rows = pl.ds(pl.multiple_of(c * KEY_CHUNK, KEY_CHUNK), KEY_CHUNK)
            s = lax.dot_general(qg, kbf_ref[0, rows, g * HD:(g + 1) * HD], (((1,), (1,)), ((), ())),
                                preferred_element_type=F32) * c2
            mask = sc_ref[c]
            for j in range(G):
                sj = s[j * QB:(j + 1) * QB] + mask
                lg_ref[c, j * QB:(j + 1) * QB, :] = sj
                for b in range(CH):
                    mx[j] = jnp.maximum(mx[j], sj[:, b * L:(b + 1) * L])
            return mx

        mx = lax.fori_loop(0, nch, logit_body, [jnp.full((QB, L), NEG_BIG, F32) for _ in range(G)])
        m = jnp.concatenate([jnp.max(x, axis=1, keepdims=True) for x in mx], axis=0)

        def value_body(c, carry):
            ls, acc = carry
            rows = pl.ds(pl.multiple_of(c * KEY_CHUNK, KEY_CHUNK), KEY_CHUNK)
            p = jnp.exp2(lg_ref[c] - m)
            for b in range(CH):
                ls = ls + p[:, b * L:(b + 1) * L]
            acc = acc + jnp.dot(p.astype(BF16), vbf_ref[0, rows, g * HD:(g + 1) * HD], preferred_element_type=F32)
            return ls, acc

        ls, acc = lax.fori_loop(0, nch, value_body, (jnp.zeros((G * QB, L), F32), jnp.zeros((G * QB, HD), F32)))
        out = acc / jnp.sum(ls, axis=1, keepdims=True)
        for j in range(G):
            o_ref[0, :, (g * G + j) * HD:(g * G + j + 1) * HD] = out[j * QB:(j + 1) * QB]


def dsa_prompt(proj, kbf, vbf, cfg: Cfg, lay: ProjLayout):
    B, T, _ = proj.shape
    QB = Q_BLOCK
    assert T % KEY_CHUNK == 0 and KEY_CHUNK % (2 * V7X_LANES) == 0 and cfg.ID * 2 == V7X_LANES and cfg.IH % 2 == 0
    topk = min(TOPK_MAX, T // 4)
    iqw, aqw, kvw = cfg.IH * cfg.ID, cfg.NH * cfg.HD, cfg.NKV * cfg.HD
    oiq, oaq, oik, oiw = (lay.offs[n] for n in ("iq", "aq", "ik", "iw"))
    assert oiq % iqw == 0 and oaq % aqw == 0
    G = cfg.NH // cfg.NKV
    kern = functools.partial(_dsa_prompt_kernel, IH=cfg.IH, ID=cfg.ID, NH=cfg.NH, G=G, HD=cfg.HD, topk=topk,
                             wscale=(cfg.IH ** -0.5) * (cfg.ID ** -0.5), sm_scale=cfg.HD ** -0.5)
    return pl.pallas_call(
        kern,
        grid=(B, T // QB),
        in_specs=[pl.BlockSpec((1, QB, iqw), lambda b, q: (b, q, oiq // iqw)),
                  pl.BlockSpec((1, QB, V7X_LANES), lambda b, q: (b, q, oiw // V7X_LANES)),
                  pl.BlockSpec((1, QB, aqw), lambda b, q: (b, q, oaq // aqw)),
                  pl.BlockSpec((1, T, V7X_LANES), lambda b, q: (b, 0, oik // V7X_LANES)),
                  pl.BlockSpec((1, T, kvw), lambda b, q: (b, 0, 0)),
                  pl.BlockSpec((1, T, kvw), lambda b, q: (b, 0, 0))],
        out_specs=pl.BlockSpec((1, QB, aqw), lambda b, q: (b, q, 0)),
        out_shape=jax.ShapeDtypeStruct((B, T, aqw), F32),
        scratch_shapes=[pltpu.VMEM((T // KEY_CHUNK, QB, KEY_CHUNK), F32),
                        pltpu.VMEM((T // KEY_CHUNK, G * QB, KEY_CHUNK), F32),
                        pltpu.VMEM((cfg.IH * QB, V7X_LANES), BF16),
                        pltpu.VMEM((cfg.IH * QB, V7X_LANES), BF16),
                        pltpu.VMEM((cfg.IH * QB, 2 * V7X_LANES), F32)],
        compiler_params=_cparams(("arbitrary", "arbitrary")),
        name="dsa_prompt",
    )(proj, proj, proj, proj, kbf, vbf)


def _dsa_sample_select_kernel(pt_ref, q_ref, w_ref, kn_ref, kidx_hbm, idx_ref, kbuf, sc_ref, rank_ref, sel_ref, sem,
                              *, NP, NPP, PG, topk, wscale):
    b = pl.program_id(0)
    nb = pl.num_programs(0)
    slot = b % 2
    L = V7X_LANES

    def page_copy(bb, sl, p):
        return pltpu.make_async_copy(kidx_hbm.at[0, pt_ref[bb, p]], kbuf.at[sl, p], sem.at[sl])

    def issue(bb, sl):
        def body(p, c):
            page_copy(bb, sl, p).start()
            return c
        lax.fori_loop(0, NP, body, 0)

    @pl.when(b == 0)
    def _():
        issue(0, 0)

    @pl.when(b + 1 < nb)
    def _():
        issue(b + 1, 1 - slot)

    def wait_body(p, c):
        page_copy(b, slot, p).wait()
        return c

    lax.fori_loop(0, NP, wait_body, 0)

    q = q_ref[0]
    w = w_ref[0][:, 0:1] * wscale
    qh, ql = _split2(q)
    dn = (((1,), (1,)), ((), ()))

    def group_body(gi, c):
        kk = kbuf[slot, pl.ds(gi * PG, PG)]
        kk = jnp.concatenate([kk[j] for j in range(PG)], axis=1)
        kh, kl = _split2(kk)
        s = (jnp.dot(qh, kh, preferred_element_type=F32) + jnp.dot(qh, kl, preferred_element_type=F32)
             + jnp.dot(ql, kh, preferred_element_type=F32))
        sc = jnp.sum(jnp.maximum(s, 0.0) * w, axis=0, keepdims=True)
        for j in range(PG):
            sc_ref[pl.ds(gi * PG + j, 1), :] = sc[:, j * L:(j + 1) * L]
        return c

    lax.fori_loop(0, NP // PG, group_body, 0)
    sn = jnp.sum(q * kn_ref[0], axis=1, keepdims=True)
    scn = jnp.sum(jnp.maximum(sn, 0.0) * w, axis=0, keepdims=True)
    tail = (NPP - NP, L)
    first = (lax.broadcasted_iota(I32, tail, 0) == 0) & (lax.broadcasted_iota(I32, tail, 1) == 0)
    sc_ref[NP:NPP, :] = jnp.where(first, scn, -jnp.inf)

    sc = sc_ref[...]
    pos = lax.broadcasted_iota(I32, sc.shape, 0) * L + lax.broadcasted_iota(I32, sc.shape, 1)

    def bit_body(i, t):
        cand = t | (jnp.int32(1) << (31 - i))
        cnt = jnp.sum(jnp.where(sc >= _offset_word_to_f32(cand), 1, 0))
        return jnp.where(cnt >= topk, cand, t)

    thr0 = _offset_word_to_f32(lax.fori_loop(0, 32, bit_body, jnp.zeros((1, L), I32)))
    n_ge = jnp.sum(jnp.where(sc >= thr0, 1, 0))
    thr = jnp.min(jnp.where(sc >= thr0, sc, jnp.inf))
    cut = jnp.int32(INT_MAX)

    def selected(thr, cut):
        return (sc > thr) | ((sc == thr) & (pos <= cut))

    def drop_last(_, tc):
        thr, cut = tc
        sel = selected(thr, cut)
        low = jnp.min(jnp.where(sel, sc, jnp.inf))
        last = jnp.max(jnp.where(sel & (sc == low), pos, -1))
        return low, last - 1

    thr, cut = lax.fori_loop(0, n_ge - topk, drop_last, (thr, cut))
    sel = selected(thr, cut)

    self_ = jnp.where(sel, 1.0, 0.0)
    li = lax.broadcasted_iota(I32, (L, L), 0)
    lj = lax.broadcasted_iota(I32, (L, L), 1)
    before_lane = jnp.where(li < lj, 1.0, 0.0).astype(BF16)
    ri = lax.broadcasted_iota(I32, (NPP, NPP), 0)
    rj = lax.broadcasted_iota(I32, (NPP, NPP), 1)
    before_row = jnp.where(rj < ri, 1.0, 0.0).astype(BF16)
    rowtot = jnp.broadcast_to(jnp.sum(self_, axis=1, keepdims=True), (NPP, L))
    rank_ref[...] = (jnp.dot(self_.astype(BF16), before_lane, preferred_element_type=F32)
                     + jnp.dot(before_row, rowtot.astype(BF16), preferred_element_type=F32))
    sel_ref[...] = self_

    slot_id = lax.broadcasted_iota(I32, (topk, 1), 0).astype(F32)
    lane1 = lax.broadcasted_iota(I32, (1, L), 1)

    def place(r, acc):
        hit = (rank_ref[pl.ds(r, 1), :] == slot_id) & (sel_ref[pl.ds(r, 1), :] > 0.0)
        return acc + jnp.where(hit, (r * L + lane1).astype(F32), 0.0)

    placed = lax.fori_loop(0, NPP, place, jnp.zeros((topk, L), F32)).astype(I32)
    ones = jnp.ones((V7X_SUBLANES, L), BF16)
    hi = (placed >> 7).astype(F32).astype(BF16)
    lo = (placed & 127).astype(F32).astype(BF16)
    idx = (lax.dot_general(ones, hi, dn, preferred_element_type=F32) * 128.0
           + lax.dot_general(ones, lo, dn, preferred_element_type=F32))
    idx_ref[0] = idx.astype(I32)


def _dsa_sample_attend_kernel(idx_s, pt_s, q_ref, idx_ref, kn_ref, vn_ref, ck_hbm, cv_hbm, o_ref, kbuf, vbuf, sem,
                              *, NKV, G, topk, past, page_shift, sm_scale):
    b = pl.program_id(0)
    nb = pl.num_programs(0)
    slot = b % 2
    page_mask = (1 << page_shift) - 1

    def row_copies(bb, sl, j):
        i = jnp.minimum(idx_s[bb, j], past - 1)
        page = pt_s[bb, i >> page_shift]
        r = i & page_mask
        return (pltpu.make_async_copy(ck_hbm.at[0, page, r], kbuf.at[sl, j], sem.at[0, sl]),
                pltpu.make_async_copy(cv_hbm.at[0, page, r], vbuf.at[sl, j], sem.at[1, sl]))

    def issue(bb, sl):
        def body(j, c):
            for cp in row_copies(bb, sl, j):
                cp.start()
            return c
        lax.fori_loop(0, topk, body, 0)

    @pl.when(b == 0)
    def _():
        issue(0, 0)

    @pl.when(b + 1 < nb)
    def _():
        issue(b + 1, 1 - slot)

    def wait_body(j, c):
        for cp in row_copies(b, slot, j):
            cp.wait()
        return c

    lax.fori_loop(0, topk, wait_body, 0)

    idx = idx_ref[0][0:1, :]
    cached = idx < past
    new_sel = jnp.max(jnp.where(idx == past, 1, 0), axis=1, keepdims=True) > 0
    for g in range(NKV):
        qg = q_ref[0][g * G:(g + 1) * G, :]
        kg = kbuf[slot, :, g, :]
        vg = vbuf[slot, :, g, :]
        s = jnp.where(cached, _dot_nt(qg, kg) * sm_scale, -jnp.inf)
        sn = jnp.sum(qg * kn_ref[0][g:g + 1, :], axis=1, keepdims=True) * sm_scale
        sn = jnp.where(new_sel, sn, -jnp.inf)
        m = jnp.maximum(jnp.max(s, axis=1, keepdims=True), sn)
        p = jnp.exp(s - m)
        pn = jnp.exp(sn - m)
        l = jnp.sum(p, axis=1, keepdims=True) + pn
        o_ref[0, g * G:(g + 1) * G, :] = (_bdot(p, vg) + pn * vn_ref[0][g:g + 1, :]) / l


def dsa_sample(q_att, q_idx, w_idx, kn_idx, k_new, v_new, cache_k, cache_v, cache_kidx, page_table, cfg: Cfg):
    DB, NP, PAGE = cfg.DB, cfg.NPAGES, cfg.PAGE
    assert PAGE == V7X_LANES and cache_k.shape[0] == 1
    total = cfg.PAST + 1
    topk = min(TOPK_MAX, total // 4)
    assert topk <= total and topk % V7X_LANES == 0
    NPP = -(-(NP + 1) // V7X_SUBLANES) * V7X_SUBLANES
    PG = math.gcd(NP, 8)
    w_b = jnp.broadcast_to(w_idx[:, :, None], (DB, cfg.IH, V7X_LANES))
    sel_kern = functools.partial(_dsa_sample_select_kernel, NP=NP, NPP=NPP, PG=PG, topk=topk,
                                 wscale=(cfg.IH ** -0.5) * (cfg.ID ** -0.5))
    idx = pl.pallas_call(
        sel_kern,
        grid_spec=pltpu.PrefetchScalarGridSpec(
            num_scalar_prefetch=1,
            grid=(DB,),
            in_specs=[pl.BlockSpec((1, cfg.IH, cfg.ID), lambda b, pt: (b, 0, 0)),
                      pl.BlockSpec((1, cfg.IH, V7X_LANES), lambda b, pt: (b, 0, 0)),
                      pl.BlockSpec((1, 1, cfg.ID), lambda b, pt: (b, 0, 0)),
                      pl.BlockSpec(memory_space=pl.ANY)],
            out_specs=pl.BlockSpec((1, V7X_SUBLANES, topk), lambda b, pt: (b, 0, 0)),
            scratch_shapes=[pltpu.VMEM((2, NP, cfg.ID, PAGE), F32),
                            pltpu.VMEM((NPP, V7X_LANES), F32),
                            pltpu.VMEM((NPP, V7X_LANES), F32),
                            pltpu.VMEM((NPP, V7X_LANES), F32),
                            pltpu.SemaphoreType.DMA((2,))]),
        out_shape=jax.ShapeDtypeStruct((DB, V7X_SUBLANES, topk), I32),
        compiler_params=_cparams(("arbitrary",)),
        name="dsa_sample_select",
    )(page_table, q_idx, w_b, kn_idx.reshape(DB, 1, cfg.ID), jnp.swapaxes(cache_kidx, 2, 3))

    G = cfg.NH // cfg.NKV
    att_kern = functools.partial(_dsa_sample_attend_kernel, NKV=cfg.NKV, G=G, topk=topk, past=cfg.PAST,
                                 page_shift=PAGE.bit_length() - 1, sm_scale=cfg.HD ** -0.5)
    return pl.pallas_call(
        att_kern,
        grid_spec=pltpu.PrefetchScalarGridSpec(
            num_scalar_prefetch=2,
            grid=(DB,),
            in_specs=[pl.BlockSpec((1, cfg.NH, cfg.HD), lambda b, ix, pt: (b, 0, 0)),
                      pl.BlockSpec((1, V7X_SUBLANES, topk), lambda b, ix, pt: (b, 0, 0)),
                      pl.BlockSpec((1, cfg.NKV, cfg.HD), lambda b, ix, pt: (b, 0, 0)),
                      pl.BlockSpec((1, cfg.NKV, cfg.HD), lambda b, ix, pt: (b, 0, 0)),
                      pl.BlockSpec(memory_space=pl.ANY),
                      pl.BlockSpec(memory_space=pl.ANY)],
            out_specs=pl.BlockSpec((1, cfg.NH, cfg.HD), lambda b, ix, pt: (b, 0, 0)),
            scratch_shapes=[pltpu.VMEM((2, topk, cfg.NKV, cfg.HD), F32),
                            pltpu.VMEM((2, topk, cfg.NKV, cfg.HD), F32),
                            pltpu.SemaphoreType.DMA((2, 2))]),
        out_shape=jax.ShapeDtypeStruct((DB, cfg.NH, cfg.HD), F32),
        compiler_params=_cparams(("arbitrary",)),
        name="dsa_sample_attend",
    )(idx[:, 0, :], page_table, q_att, idx, k_new.reshape(DB, cfg.NKV, cfg.HD), v_new.reshape(DB, cfg.NKV, cfg.HD),
      cache_k, cache_v)


def _merge_kernel(og_ref, oa_ref, wg_ref, wa_ref, za_ref, zb_ref, o_ref, ogb_ref, oab_ref):
    @pl.when(pl.program_id(1) == 0)
    def _():
        ogb_ref[...] = og_ref[...].astype(BF16)
        oab_ref[...] = oa_ref[...].astype(BF16)

    a = jnp.dot(ogb_ref[...], wg_ref[...].astype(BF16), preferred_element_type=F32)
    b = jnp.dot(oab_ref[...], wa_ref[...].astype(BF16), preferred_element_type=F32)
    o_ref[...] = _sigmoid(za_ref[...]) * a + _sigmoid(zb_ref[...]) * b


def merge_branches(og, oa, w_gla_o, w_attn_o, proj, cfg: Cfg, lay: ProjLayout, tm):
    N = og.shape[0]
    D, tn = cfg.D, lay.tn
    oza, ozb = lay.offs["za"] // tn, lay.offs["zb"] // tn
    return pl.pallas_call(
        _merge_kernel,
        grid=(N // tm, D // tn),
        in_specs=[pl.BlockSpec((tm, og.shape[1]), lambda i, j: (i, 0)),
                  pl.BlockSpec((tm, oa.shape[1]), lambda i, j: (i, 0)),
                  pl.BlockSpec((og.shape[1], tn), lambda i, j: (0, j)),
                  pl.BlockSpec((oa.shape[1], tn), lambda i, j: (0, j)),
                  pl.BlockSpec((tm, tn), lambda i, j: (i, oza + j)),
                  pl.BlockSpec((tm, tn), lambda i, j: (i, ozb + j))],
        out_specs=pl.BlockSpec((tm, tn), lambda i, j: (i, j)),
        out_shape=jax.ShapeDtypeStruct((N, D), F32),
        scratch_shapes=[pltpu.VMEM((tm, og.shape[1]), BF16), pltpu.VMEM((tm, oa.shape[1]), BF16)],
        compiler_params=_cparams(("arbitrary", "arbitrary")),
        name="merge_branches",
    )(og, oa, w_gla_o, w_attn_o, proj, proj)


def _outproj_kernel(m_ref, w_ref, x_ref, gt_ref, o_ref, mb_ref):
    @pl.when(pl.program_id(1) == 0)
    def _():
        mb_ref[...] = m_ref[...].astype(BF16)

    o_ref[...] = x_ref[...] + gt_ref[...] * jnp.dot(mb_ref[...], w_ref[...].astype(BF16), preferred_element_type=F32)


def out_proj(merged, w_out, x, gt, tm, rows_per_group, tn):
    N, D = x.shape
    G, R, _ = gt.shape
    tpg = rows_per_group // tm
    return pl.pallas_call(
        _outproj_kernel,
        grid=(N // tm, D // tn),
        in_specs=[pl.BlockSpec((tm, D), lambda i, j: (i, 0)),
                  pl.BlockSpec((D, tn), lambda i, j: (0, j)),
                  pl.BlockSpec((tm, tn), lambda i, j: (i, j)),
                  pl.BlockSpec((None, R, tn), lambda i, j: (i // tpg, 0, j))],
        out_specs=pl.BlockSpec((tm, tn), lambda i, j: (i, j)),
        out_shape=jax.ShapeDtypeStruct((N, D), F32),
        scratch_shapes=[pltpu.VMEM((tm, D), BF16)],
        compiler_params=_cparams(("arbitrary", "arbitrary")),
        name="out_proj",
    )(merged, w_out, x, gt)


def _ffn_pre_kernel(x_ref, g_ref, sc_ref, sh_ref, wrt_ref, br_ref, wg_ref, wu_ref, wd_ref,
                    h2_ref, ysh_ref, eidx_ref, wts_ref,
                    wrh_ref, wrl_ref, wgb_ref, wub_ref, wdb_ref, *, E, NG, first_tile):
    @pl.when(pl.program_id(0) == first_tile)
    def _():
        hi, lo = _split2(wrt_ref[...])
        wrh_ref[...] = hi
        wrl_ref[...] = lo
        wgb_ref[...] = wg_ref[...].astype(BF16)
        wub_ref[...] = wu_ref[...].astype(BF16)
        wdb_ref[...] = wd_ref[...].astype(BF16)

    x = x_ref[...]
    tm = x.shape[0]
    ms = jnp.mean(x * x, axis=-1, keepdims=True)
    h2 = x * lax.rsqrt(ms + NORM_EPS) * g_ref[...] * (1.0 + sc_ref[...]) + sh_ref[...]
    h2_ref[...] = h2
    hh, hl = _split2(h2)

    gate = jnp.dot(hh, wgb_ref[...], preferred_element_type=F32)
    up = jnp.dot(hh, wub_ref[...], preferred_element_type=F32)
    ysh_ref[...] = jnp.dot((_silu(gate) * up).astype(BF16), wdb_ref[...], preferred_element_type=F32)

    dn = (((1,), (1,)), ((), ()))
    logit = (lax.dot_general(wrh_ref[...], hh, dn, preferred_element_type=F32)
             + lax.dot_general(wrh_ref[...], hl, dn, preferred_element_type=F32)
             + lax.dot_general(wrl_ref[...], hh, dn, preferred_element_type=F32))
    s = _sigmoid(logit)
    PG = E // NG
    s3 = s.reshape(NG, PG, tm)
    sel3 = (s + br_ref[:, 0:1]).reshape(NG, PG, tm)
    ninf = -jnp.inf
    ipg = lax.broadcasted_iota(I32, (NG, PG, tm), 1)
    m1 = jnp.max(sel3, axis=1, keepdims=True)
    i1 = jnp.min(jnp.where(sel3 == m1, ipg, PG), axis=1, keepdims=True)
    m2 = jnp.max(jnp.where(ipg == i1, ninf, sel3), axis=1, keepdims=True)
    gs = (m1 + m2).reshape(NG, tm)
    ig = lax.broadcasted_iota(I32, (NG, tm), 0)
    gmask = jnp.zeros((NG, tm), jnp.bool_)
    for _ in range(TOPK_GROUPS):
        m = jnp.max(gs, axis=0, keepdims=True)
        i = jnp.min(jnp.where(gs == m, ig, NG), axis=0, keepdims=True)
        pick = ig == i
        gmask = gmask | pick
        gs = jnp.where(pick, ninf, gs)
    cand = jnp.where(gmask.reshape(NG, 1, tm), sel3, ninf)
    ie = lax.broadcasted_iota(I32, (NG, PG, tm), 0) * PG + ipg
    idxs, ws = [], []
    for _ in range(TOP_K):
        m = jnp.max(jnp.max(cand, axis=0, keepdims=True), axis=1, keepdims=True)
        hit = jnp.where(cand == m, ie, E)
        i = jnp.min(jnp.min(hit, axis=0, keepdims=True), axis=1, keepdims=True)
        pick = ie == i
        wsel = jnp.where(pick, s3, 0.0)
        ws.append(jnp.sum(jnp.sum(wsel, axis=0, keepdims=True), axis=1, keepdims=True).reshape(1, tm))
        idxs.append(i.reshape(1, tm))
        cand = jnp.where(pick, ninf, cand)
    tot = ws[0]
    for w in ws[1:]:
        tot = tot + w
    for k in range(TOP_K):
        eidx_ref[k:k + 1, :] = idxs[k]
        wts_ref[k:k + 1, :] = ws[k] / tot * ROUTED_SCALE


def ffn_pre(x1, g, sc, sh, w_router_t, b_router, w_sh_gate, w_sh_up, w_sh_down, cfg: Cfg, tm):
    N, D = x1.shape
    G, R, _ = sc.shape
    tpg = (N // G) // tm
    E, DE = cfg.E, cfg.DE
    assert E // N_GROUPS == V7X_SUBLANES
    mod_spec = pl.BlockSpec((None, R, D), lambda i: (i // tpg, 0, 0))
    const = lambda shape: pl.BlockSpec(shape, lambda i: (0,) * len(shape))
    return pl.pallas_call(
        functools.partial(_ffn_pre_kernel, E=E, NG=N_GROUPS, first_tile=0),
        grid=(N // tm,),
        in_specs=[pl.BlockSpec((tm, D), lambda i: (i, 0)), const((1, D)), mod_spec, mod_spec,
                  const((E, D)), const((E, V7X_LANES)), const((D, DE)), const((D, DE)), const((DE, D))],
        out_specs=[pl.BlockSpec((tm, D), lambda i: (i, 0)),
                   pl.BlockSpec((tm, D), lambda i: (i, 0)),
                   pl.BlockSpec((TOP_K, tm), lambda i: (0, i)),
                   pl.BlockSpec((TOP_K, tm), lambda i: (0, i))],
        out_shape=[jax.ShapeDtypeStruct((N, D), F32),
                   jax.ShapeDtypeStruct((N, D), F32),
                   jax.ShapeDtypeStruct((TOP_K, N), I32),
                   jax.ShapeDtypeStruct((TOP_K, N), F32)],
        scratch_shapes=[pltpu.VMEM((E, D), BF16), pltpu.VMEM((E, D), BF16),
                        pltpu.VMEM((D, DE), BF16), pltpu.VMEM((D, DE), BF16), pltpu.VMEM((DE, D), BF16)],
        compiler_params=_cparams(("arbitrary",)),
        name="ffn_pre",
    )(x1, g.reshape(1, D), sc, sh, w_router_t, jnp.broadcast_to(b_router[:, None], (E, V7X_LANES)),
      w_sh_gate, w_sh_up, w_sh_down)


def _rank_kernel(eidx_ref, rank_ref, cnt_ref, carry_ref, *, E):
    i = pl.program_id(0)

    @pl.when(i == 0)
    def _():
        carry_ref[...] = jnp.zeros(carry_ref.shape, F32)

    eidx = eidx_ref[...]
    tm = eidx.shape[1]
    eio = lax.broadcasted_iota(I32, (E, 1), 0)
    onehot = jnp.zeros((E, tm), F32)
    for k in range(TOP_K):
        onehot = onehot + jnp.where(eidx[k:k + 1, :] == eio, 1.0, 0.0)
    r = lax.broadcasted_iota(I32, (tm, tm), 0)
    c = lax.broadcasted_iota(I32, (tm, tm), 1)
    tri = jnp.where(r <= c, 1.0, 0.0).astype(BF16)
    cum = jnp.dot(onehot.astype(BF16), tri, preferred_element_type=F32)
    excl = cum - onehot + carry_ref[:, 0:1]
    for k in range(TOP_K):
        hit = eidx[k:k + 1, :] == eio
        rank_ref[k:k + 1, :] = jnp.sum(jnp.where(hit, excl, 0.0), axis=0, keepdims=True).astype(I32)
    carry_ref[...] = carry_ref[...] + cum[:, tm - 1:tm]

    @pl.when(i == pl.num_programs(0) - 1)
    def _():
        cnt_ref[...] = carry_ref[...].astype(I32)


def _plan_kernel(eidx_ref, rank_ref, cnt_ref, dest_ref, be_ref, nr_ref, *, E, bm, NBP):
    L = V7X_LANES
    sh = bm.bit_length() - 1
    cnt = cnt_ref[...]
    padded = ((cnt + (bm - 1)) >> sh) << sh
    r = lax.broadcasted_iota(I32, (E, E), 0)
    c = lax.broadcasted_iota(I32, (E, E), 1)
    low = jnp.where(c < r, 1.0, 0.0).astype(BF16)
    pstart = jnp.zeros((E, L), F32)
    for shift in (0, 8, 16):
        piece = ((padded >> shift) & 255).astype(F32).astype(BF16)
        pstart = pstart + jnp.dot(low, piece, preferred_element_type=F32) * float(1 << shift)
    pstart = pstart.astype(I32)
    pend = pstart + padded
    eio = lax.broadcasted_iota(I32, (E, 1), 0)
    eidx = eidx_ref[...]
    ps_col = pstart[:, 0:1]
    for k in range(TOP_K):
        e = eidx[k:k + 1, :]
        base = jnp.sum(jnp.where(e == eio, ps_col, 0), axis=0, keepdims=True)
        dest_ref[k:k + 1, :] = jnp.where(e >= 0, base + rank_ref[k:k + 1, :], -1)

    @pl.when(pl.program_id(0) == 0)
    def _():
        blk0 = lax.broadcasted_iota(I32, (1, NBP), 1) * bm
        be = jnp.sum((pend[:, 0:1] <= blk0).astype(I32), axis=0, keepdims=True)
        be = jnp.minimum(be, E - 1)
        last = jnp.sum(jnp.where(eio == be, (pstart + cnt)[:, 0:1], 0), axis=0, keepdims=True)
        be_ref[...] = be
        nr_ref[...] = jnp.clip(last - blk0, 0, bm)


def moe_plan(eidx, cfg: Cfg, bm, tm):
    K, N = eidx.shape
    E = cfg.E
    NB = -(-(N * K) // bm) + E
    NBP = -(-NB // V7X_LANES) * V7X_LANES
    rank, cnt = pl.pallas_call(
        functools.partial(_rank_kernel, E=E),
        grid=(N // tm,),
        in_specs=[pl.BlockSpec((K, tm), lambda i: (0, i))],
        out_specs=[pl.BlockSpec((K, tm), lambda i: (0, i)), pl.BlockSpec((E, V7X_LANES), lambda i: (0, 0))],
        out_shape=[jax.ShapeDtypeStruct((K, N), I32), jax.ShapeDtypeStruct((E, V7X_LANES), I32)],
        scratch_shapes=[pltpu.VMEM((E, V7X_LANES), F32)],
        compiler_params=_cparams(("arbitrary",)),
        name="moe_rank",
    )(eidx)
    dest, be, nr = pl.pallas_call(
        functools.partial(_plan_kernel, E=E, bm=bm, NBP=NBP),
        grid=(N // tm,),
        in_specs=[pl.BlockSpec((K, tm), lambda i: (0, i)), pl.BlockSpec((K, tm), lambda i: (0, i)),
                  pl.BlockSpec((E, V7X_LANES), lambda i: (0, 0))],
        out_specs=[pl.BlockSpec((K, tm), lambda i: (0, i)), pl.BlockSpec((1, NBP), lambda i: (0, 0)),
                   pl.BlockSpec((1, NBP), lambda i: (0, 0))],
        out_shape=[jax.ShapeDtypeStruct((K, N), I32), jax.ShapeDtypeStruct((1, NBP), I32),
                   jax.ShapeDtypeStruct((1, NBP), I32)],
        compiler_params=_cparams(("arbitrary",)),
        name="moe_plan",
    )(eidx, rank, cnt)
    return dest, be.reshape(NBP), nr.reshape(NBP), NB


def _dispatch_kernel(dest_ref, hp_ref, hs_ref, xs_ref, sem, *, tm, tiles_p):
    i = pl.program_id(0)

    def scatter_rows(h_ref, rows):
        def row_copy(t, k):
            return pltpu.make_async_copy(h_ref.at[pl.ds(t, 1)], xs_ref.at[pl.ds(dest_ref[k, t], 1)], sem)

        def start(t, c):
            for k in range(TOP_K):
                @pl.when(dest_ref[k, t] >= 0)
                def _():
                    row_copy(t, k).start()
            return c

        def wait(t, c):
            for k in range(TOP_K):
                @pl.when(dest_ref[k, t] >= 0)
                def _():
                    row_copy(t, k).wait()
            return c

        lax.fori_loop(0, rows, start, 0)
        lax.fori_loop(0, rows, wait, 0)

    @pl.when(i < tiles_p)
    def _():
        scatter_rows(hp_ref, tm)

    @pl.when(i >= tiles_p)
    def _():
        scatter_rows(hs_ref, hs_ref.shape[0])


def moe_dispatch(h2_p, h2_s, dest, P, tm):
    D = h2_p.shape[1]
    NS = h2_s.shape[0]
    assert h2_p.shape[0] % tm == 0 and NS <= tm
    tiles_p = h2_p.shape[0] // tm
    assert dest.shape[1] == (tiles_p + 1) * tm
    return pl.pallas_call(
        functools.partial(_dispatch_kernel, tm=tm, tiles_p=tiles_p),
        grid=(tiles_p + 1,),
        in_specs=[pl.BlockSpec((TOP_K, tm), lambda i: (0, i), memory_space=pltpu.SMEM),
                  pl.BlockSpec((tm, D), lambda i: (jnp.minimum(i, tiles_p - 1), 0)),
                  pl.BlockSpec((NS, D), lambda i: (0, 0))],
        out_specs=pl.BlockSpec(memory_space=pl.ANY),
        out_shape=jax.ShapeDtypeStruct((P, D), F32),
        scratch_shapes=[pltpu.SemaphoreType.DMA(())],
        compiler_params=_cparams(("arbitrary",)),
        name="moe_dispatch",
    )(dest, h2_p, h2_s)


def _experts_kernel(be_ref, nr_ref, x_ref, wg_ref, wu_ref, wd_ref, y_ref, wgb_ref, wub_ref, wdb_ref):
    b = pl.program_id(0)
    prev = be_ref[jnp.maximum(b - 1, 0)]
    n = nr_ref[b]

    @pl.when((n > 0) & ((b == 0) | (be_ref[b] != prev)))
    def _():
        wgb_ref[...] = wg_ref[0].astype(BF16)
        wub_ref[...] = wu_ref[0].astype(BF16)
        wdb_ref[...] = wd_ref[0].astype(BF16)

    @pl.when(n > 0)
    def _():
        row = lax.broadcasted_iota(I32, (x_ref.shape[0], 1), 0)
        x = jnp.where(row < n, x_ref[...], 0.0).astype(BF16)
        gate = jnp.dot(x, wgb_ref[...], preferred_element_type=F32)
        up = jnp.dot(x, wub_ref[...], preferred_element_type=F32)
        y_ref[...] = jnp.dot((_silu(gate) * up).astype(BF16), wdb_ref[...], preferred_element_type=F32)

    @pl.when(n == 0)
    def _():
        y_ref[...] = jnp.zeros(y_ref.shape, F32)


def moe_experts(xs, be, nr, w_gate, w_up, w_down, NB, bm):
    P, D = xs.shape
    E, _, DE = w_gate.shape
    return pl.pallas_call(
        _experts_kernel,
        grid_spec=pltpu.PrefetchScalarGridSpec(
            num_scalar_prefetch=2,
            grid=(NB,),
            in_specs=[pl.BlockSpec((bm, D), lambda b, be, nr: (b, 0)),
                      pl.BlockSpec((1, D, DE), lambda b, be, nr: (be[b], 0, 0)),
                      pl.BlockSpec((1, D, DE), lambda b, be, nr: (be[b], 0, 0)),
                      pl.BlockSpec((1, DE, D), lambda b, be, nr: (be[b], 0, 0))],
            out_specs=pl.BlockSpec((bm, D), lambda b, be, nr: (b, 0)),
            scratch_shapes=[pltpu.VMEM((D, DE), BF16), pltpu.VMEM((D, DE), BF16), pltpu.VMEM((DE, D), BF16)]),
        out_shape=jax.ShapeDtypeStruct((P, D), F32),
        compiler_params=_cparams(("arbitrary",)),
        name="moe_experts",
    )(be, nr, xs, w_gate, w_up, w_down)


def _combine_kernel(dest_ref, ys_ref, wts_ref, x_ref, ysh_ref, gt_ref, gf_ref, o_ref, ybuf_ref, sem, *, tm):
    def row_copy(t, k):
        return pltpu.make_async_copy(ys_ref.at[pl.ds(dest_ref[k, t], 1)], ybuf_ref.at[k, pl.ds(t, 1)], sem)

    def start(t, c):
        for k in range(TOP_K):
            row_copy(t, k).start()
        return c

    def wait(t, c):
        for k in range(TOP_K):
            row_copy(t, k).wait()
        return c

    lax.fori_loop(0, tm, start, 0)
    lax.fori_loop(0, tm, wait, 0)
    w = wts_ref[...]
    ffn = ysh_ref[...]
    for k in range(TOP_K):
        ffn = ffn + w[:, k:k + 1] * ybuf_ref[k]
    x2 = x_ref[...] + gt_ref[...] * ffn
    ms = jnp.mean(x2 * x2, axis=-1, keepdims=True)
    o_ref[...] = x2 * lax.rsqrt(ms + NORM_EPS) * gf_ref[...]


def moe_combine(ys, dest, wts_t, x1, ysh, gt, g_final, tm, rows_per_group):
    N, D = x1.shape
    G, R, _ = gt.shape
    tpg = rows_per_group // tm
    return pl.pallas_call(
        functools.partial(_combine_kernel, tm=tm),
        grid=(N // tm,),
        in_specs=[pl.BlockSpec((TOP_K, tm), lambda i: (0, i), memory_space=pltpu.SMEM),
                  pl.BlockSpec(memory_space=pl.ANY),
                  pl.BlockSpec((tm, TOP_K), lambda i: (i, 0)),
                  pl.BlockSpec((tm, D), lambda i: (i, 0)),
                  pl.BlockSpec((tm, D), lambda i: (i, 0)),
                  pl.BlockSpec((None, R, D), lambda i: (i // tpg, 0, 0)),
                  pl.BlockSpec((1, D), lambda i: (0, 0))],
        out_specs=pl.BlockSpec((tm, D), lambda i: (i, 0)),
        out_shape=jax.ShapeDtypeStruct((N, D), F32),
        scratch_shapes=[pltpu.VMEM((TOP_K, tm, D), F32), pltpu.SemaphoreType.DMA(())],
        compiler_params=_cparams(("arbitrary",)),
        name="moe_combine",
    )(dest, ys, wts_t, x1, ysh, gt, g_final.reshape(1, D))


def make_cfg(x_prompt, x_sample, cache_k, cache_kidx, state_gla, page_table, w_in, w_gla_a, w_router, w_exp_gate):
    B, T, D = x_prompt.shape
    DB = x_sample.shape[0]
    assert x_sample.shape[1] == 1 and w_in.shape[0] == 1
    _, _, PAGE, NKV, HD = cache_k.shape
    ID = cache_kidx.shape[-1]
    _, _, GH, DK, DV = state_gla.shape
    RANK = w_gla_a.shape[1]
    NH = D // HD
    rest = w_in.shape[2] - (2 * GH * DK + 2 * GH * DV + RANK + NH * HD + 2 * NKV * HD + ID + 2 * D)
    IH = rest // (ID + 1)
    assert IH * (ID + 1) == rest
    NPAGES = page_table.shape[1]
    return Cfg(D=D, B=B, T=T, DB=DB, GH=GH, DK=DK, DV=DV, RANK=RANK, NH=NH, NKV=NKV, HD=HD, IH=IH, ID=ID,
               E=w_router.shape[2], DE=w_exp_gate.shape[3], PAGE=PAGE, NPAGES=NPAGES, PAST=NPAGES * PAGE)


def _row_tile(n, cap):
    t = cap
    while n % t:
        t //= 2
    return t


def kernel(x_prompt, x_sample, c_prompt, c_sample, cache_k, cache_v, cache_kidx, state_gla, page_table, w_ada, b_ada,
           g_norm1, w_in, w_gla_a, b_gla_a, g_gla_norm, w_gla_o, w_attn_o, w_out, g_norm2, w_router, b_router,
           w_sh_gate, w_sh_up, w_sh_down, w_exp_gate, w_exp_up, w_exp_down, g_final):
    cfg = make_cfg(x_prompt, x_sample, cache_k, cache_kidx, state_gla, page_table, w_in, w_gla_a, w_router, w_exp_gate)
    lay = proj_layout(cfg)
    B, T, D, DB = cfg.B, cfg.T, cfg.D, cfg.DB
    NP_, NS = B * T, DB
    kvw = cfg.NKV * cfg.HD

    pad = (-(B + DB)) % V7X_SUBLANES
    c_all = jnp.concatenate([c_prompt, c_sample, jnp.zeros((pad, D), F32)], axis=0)
    mod = ada_mod(c_all, w_ada[0], b_ada[0])
    sh1p, sc1p, gt1p, sh2p, sc2p, gt2p = [m[:, None, :] for m in jnp.split(mod[:B], 6, axis=-1)]
    sh1s, sc1s, gt1s, sh2s, sc2s, gt2s = [m[None] for m in jnp.split(mod[B:B + DB], 6, axis=-1)]

    w_pad = relayout_w_in(w_in[0], cfg, lay)
    tm_p = _row_tile(T, 1024)
    xp = x_prompt.reshape(NP_, D)
    xs = x_sample.reshape(NS, D)
    proj_p = in_proj(xp, g_norm1[0], sc1p, sh1p, w_pad, rope_tables(np.arange(T), cfg), cfg, lay, tm_p, T)
    proj_s = in_proj(xs, g_norm1[0], sc1s, sh1s, w_pad, rope_tables(np.full((NS,), cfg.PAST), cfg), cfg, lay, NS, NS)

    def grp(proj, n, w=None):
        return proj[:, lay.offs[n]:lay.offs[n] + (lay.widths[n] if w is None else w)]

    proj_p3 = proj_p.reshape(B, T, lay.total)
    og_p, gla_p = gla_prompt(proj_p3, w_gla_a[0], b_gla_a[0], g_gla_norm[0], cfg, lay)
    og_s, gla_s = gla_sample(proj_s.reshape(DB, 1, lay.total), state_gla[0], w_gla_a[0], b_gla_a[0], g_gla_norm[0],
                             cfg, lay)

    k_p, v_p, kidx_p = grp(proj_p, "ak"), grp(proj_p, "av"), grp(proj_p, "ik", cfg.ID)
    k_s, v_s, kidx_s = grp(proj_s, "ak"), grp(proj_s, "av"), grp(proj_s, "ik", cfg.ID)
    oa_p = dsa_prompt(proj_p3, k_p.astype(BF16).reshape(B, T, kvw), v_p.astype(BF16).reshape(B, T, kvw), cfg, lay)
    oa_s = dsa_sample(grp(proj_s, "aq").reshape(DB, cfg.NH, cfg.HD), grp(proj_s, "iq").reshape(DB, cfg.IH, cfg.ID),
                      grp(proj_s, "iw", cfg.IH), kidx_s, k_s, v_s, cache_k, cache_v, cache_kidx, page_table, cfg)

    tm_m = _row_tile(T, 512)
    mg_p = merge_branches(og_p.reshape(NP_, -1), oa_p.reshape(NP_, -1), w_gla_o[0], w_attn_o[0], proj_p, cfg, lay, tm_m)
    mg_s = merge_branches(og_s.reshape(NS, -1), oa_s.reshape(NS, -1), w_gla_o[0], w_attn_o[0], proj_s, cfg, lay, NS)
    x1_p = out_proj(mg_p, w_out[0], xp, gt1p, tm_m, T, lay.tn)
    x1_s = out_proj(mg_s, w_out[0], xs, gt1s, NS, NS, lay.tn)

    tile = V7X_LANES
    assert NP_ % tile == 0 and NS <= tile
    n_tok = NP_ + tile
    wr_t = w_router[0].T
    tm_f = _row_tile(T, 256)
    h2_p, ysh_p, eidx_p, wts_p = ffn_pre(x1_p, g_norm2[0], sc2p, sh2p, wr_t, b_router[0], w_sh_gate[0], w_sh_up[0],
                                         w_sh_down[0], cfg, tm_f)
    h2_s, ysh_s, eidx_s, wts_s = ffn_pre(x1_s, g_norm2[0], sc2s, sh2s, wr_t, b_router[0], w_sh_gate[0], w_sh_up[0],
                                         w_sh_down[0], cfg, NS)
    eidx = jnp.concatenate([eidx_p, eidx_s, jnp.full((TOP_K, n_tok - NP_ - NS), -1, I32)], axis=1)
    bm = 256 if NP_ * TOP_K >= 256 * cfg.E * 4 else 128
    dest, be, nr, NB = moe_plan(eidx, cfg, bm, tile)
    xs_sorted = moe_dispatch(h2_p, h2_s, dest, NB * bm, tile)
    ys_sorted = moe_experts(xs_sorted, be, nr, w_exp_gate[0], w_exp_up[0], w_exp_down[0], NB, bm)
    tm_c = _row_tile(T, 128)
    y_p = moe_combine(ys_sorted, dest[:, :NP_], wts_p.T, x1_p, ysh_p, gt2p, g_final, tm_c, T)
    y_s = moe_combine(ys_sorted, dest[:, NP_:NP_ + NS], wts_s.T, x1_s, ysh_s, gt2s, g_final, NS, NS)

    return (y_p.reshape(B, T, D), y_s.reshape(DB, 1, D),
            k_p.reshape(1, B, T, cfg.NKV, cfg.HD), v_p.reshape(1, B, T, cfg.NKV, cfg.HD), kidx_p.reshape(1, B, T, cfg.ID),
            gla_p[None],
            k_s.reshape(1, DB, 1, cfg.NKV, cfg.HD), v_s.reshape(1, DB, 1, cfg.NKV, cfg.HD), kidx_s.reshape(1, DB, 1, cfg.ID),
            gla_s[None])
```

```python
import functools
import math
from typing import NamedTuple

import jax
import jax.numpy as jnp
import numpy as np
from jax import lax
from jax.experimental import pallas as pl
from jax.experimental.pallas import tpu as pltpu

F32 = jnp.float32
BF16 = jnp.bfloat16
I32 = jnp.int32

V7X_LANES = 128
V7X_SUBLANES = 8
V7X_VMEM_LIMIT_BYTES = 56 * 1024 * 1024

GLA_TAU = 16.0
GLA_CHUNK = 64
ROPE_THETA = 500000.0
TOPK_MAX = 256
Q_BLOCK = 128
KEY_CHUNK = 1024
DMA_LOOP_UNROLL = 4
N_GROUPS = 8
TOPK_GROUPS = 4
TOP_K = 8
ROUTED_SCALE = 2.5
NORM_EPS = 1e-6
INT_MIN = -(2 ** 31)
INT_MAX = 2 ** 31 - 1
NEG_BIG = -1e30


class Cfg(NamedTuple):
    D: int
    B: int
    T: int
    DB: int
    GH: int
    DK: int
    DV: int
    RANK: int
    NH: int
    NKV: int
    HD: int
    IH: int
    ID: int
    E: int
    DE: int
    PAGE: int
    NPAGES: int
    PAST: int


def _cparams(sem, vmem=V7X_VMEM_LIMIT_BYTES):
    return pltpu.CompilerParams(dimension_semantics=sem, vmem_limit_bytes=vmem)


def _bdot(a, b):
    return jnp.dot(a.astype(BF16), b.astype(BF16), preferred_element_type=F32)


def _dot_nt(a, b):
    return lax.dot_general(a.astype(BF16), b.astype(BF16), (((1,), (1,)), ((), ())),
                           preferred_element_type=F32)


def _dot_tn(a, b):
    return lax.dot_general(a.astype(BF16), b.astype(BF16), (((0,), (0,)), ((), ())),
                           preferred_element_type=F32)


def _split2(x):
    hi = x.astype(BF16)
    lo = (x - hi.astype(F32)).astype(BF16)
    return hi, lo


def _split3(x):
    hi = x.astype(BF16)
    r1 = x - hi.astype(F32)
    mid = r1.astype(BF16)
    lo = (r1 - mid.astype(F32)).astype(BF16)
    return hi, mid, lo


def _sigmoid(x):
    return 1.0 / (1.0 + jnp.exp(-x))


def _silu(x):
    return x * _sigmoid(x)


def _ada_kernel(c_ref, w_ref, b_ref, o_ref):
    c = c_ref[...]
    o_ref[...] = _bdot(_silu(c), w_ref[...]) + b_ref[...]


def ada_mod(c, w_ada, b_ada):
    R, D = c.shape
    N = w_ada.shape[1]
    tn = 512 if N % 512 == 0 else 128
    return pl.pallas_call(
        _ada_kernel,
        grid=(N // tn,),
        in_specs=[pl.BlockSpec((R, D), lambda j: (0, 0)),
                  pl.BlockSpec((D, tn), lambda j: (0, j)),
                  pl.BlockSpec((1, tn), lambda j: (0, j))],
        out_specs=pl.BlockSpec((R, tn), lambda j: (0, j)),
        out_shape=jax.ShapeDtypeStruct((R, N), F32),
        compiler_params=_cparams(("arbitrary",)),
        name="ada_mod",
    )(c, w_ada, b_ada.reshape(1, N))


class ProjLayout(NamedTuple):
    names: tuple
    offs: dict
    widths: dict
    total: int
    tn: int
    chunk_types: tuple


def proj_layout(cfg: Cfg) -> ProjLayout:
    qk = cfg.GH * cfg.DK
    vv = cfg.GH * cfg.DV
    aq = cfg.NH * cfg.HD
    kv = cfg.NKV * cfg.HD
    iq = cfg.IH * cfg.ID
    big = [("gq", qk), ("gk", qk), ("gv", vv), ("gr", vv), ("aq", aq), ("ak", kv), ("av", kv),
           ("iq", iq), ("za", cfg.D), ("zb", cfg.D)]
    tn = 512
    for _, w in big:
        tn = math.gcd(tn, w)
    assert tn % V7X_LANES == 0
    offs, widths, types = {}, {}, []
    off = 0
    for n, w in big:
        offs[n], widths[n] = off, w
        t = 1 if n in ("aq", "ak") else (2 if n == "iq" else 0)
        types += [t] * (w // V7X_LANES)
        off += w
    for n, t in (("ga", 0), ("ik", 2), ("iw", 0), ("pad", 0)):
        offs[n], widths[n] = off, V7X_LANES
        types.append(t)
        off += V7X_LANES
    assert off % tn == 0
    return ProjLayout(tuple(n for n, _ in big) + ("ga", "ik", "iw", "pad"), offs, widths, off, tn, tuple(types))


def relayout_w_in(w_in, cfg: Cfg, lay: ProjLayout):
    qk = cfg.GH * cfg.DK
    vv = cfg.GH * cfg.DV
    aq = cfg.NH * cfg.HD
    kv = cfg.NKV * cfg.HD
    iq = cfg.IH * cfg.ID
    src_w = dict(gq=qk, gk=qk, gv=vv, ga=cfg.RANK, gr=vv, aq=aq, ak=kv, av=kv, iq=iq, ik=cfg.ID, iw=cfg.IH,
                 za=cfg.D, zb=cfg.D)
    src_off, o = {}, 0
    for n in ("gq", "gk", "gv", "ga", "gr", "aq", "ak", "av", "iq", "ik", "iw", "za", "zb"):
        src_off[n] = o
        o += src_w[n]
    assert o == w_in.shape[1]
    D = w_in.shape[0]

    def cols(n):
        return w_in[:, src_off[n]:src_off[n] + src_w[n]]

    def padded(a):
        return jnp.pad(a, ((0, 0), (0, V7X_LANES - a.shape[1])))

    assert 2 * cfg.ID == V7X_LANES and cfg.RANK <= V7X_LANES and cfg.IH <= V7X_LANES
    parts = [cols(n) for n in ("gq", "gk", "gv", "gr", "aq", "ak", "av", "iq", "za", "zb")]
    parts += [padded(cols("ga")), cols("ik"), cols("ik"), padded(cols("iw")), jnp.zeros((D, V7X_LANES), w_in.dtype)]
    return jnp.concatenate(parts, axis=1).astype(BF16)


def rope_tables(pos, cfg: Cfg):
    pos = np.asarray(pos, np.float64)

    def one(width, rot):
        half = rot // 2
        inv = ROPE_THETA ** (-np.arange(half, dtype=np.float64) * 2.0 / rot)
        ang = pos[:, None] * inv[None, :]
        lane = np.arange(V7X_LANES) % width
        c = np.where(lane < rot, np.cos(ang)[:, lane % half], 1.0)
        s1 = np.where((lane >= half) & (lane < rot), np.sin(ang)[:, (lane - half) % half], 0.0)
        s2 = np.where(lane < half, -np.sin(ang)[:, lane % half], 0.0)
        return [c, s1, s2]

    tabs = one(cfg.HD, cfg.HD // 4) + one(cfg.ID, cfg.ID // 4)
    return jnp.asarray(np.concatenate(tabs, axis=1), F32)


def _inproj_kernel(x_ref, g_ref, sc_ref, sh_ref, w_ref, rope_ref, o_ref, h_ref, *, patterns, half_a, half_i):
    j = pl.program_id(1)

    @pl.when(j == 0)
    def _():
        x = x_ref[...]
        ms = jnp.mean(x * x, axis=-1, keepdims=True)
        y = x * lax.rsqrt(ms + NORM_EPS) * g_ref[...]
        h_ref[...] = (y * (1.0 + sc_ref[...]) + sh_ref[...]).astype(BF16)

    acc = jnp.dot(h_ref[...], w_ref[...], preferred_element_type=F32)
    L = V7X_LANES

    def rot(a, t):
        base, half = (0, half_a) if t == 1 else (3 * L, half_i)
        c = rope_ref[:, base:base + L]
        s1 = rope_ref[:, base + L:base + 2 * L]
        s2 = rope_ref[:, base + 2 * L:base + 3 * L]
        return a * c + pltpu.roll(a, half, 1) * s1 + pltpu.roll(a, L - half, 1) * s2

    for pat, ranges in patterns:
        cond = None
        for lo, hi in ranges:
            c = (j >= lo) & (j < hi)
            cond = c if cond is None else (cond | c)

        @pl.when(cond)
        def _(pat=pat):
            if all(t == 0 for t in pat):
                o_ref[...] = acc
            else:
                for ci, t in enumerate(pat):
                    a = acc[:, ci * L:(ci + 1) * L]
                    o_ref[:, ci * L:(ci + 1) * L] = a if t == 0 else rot(a, t)


def _tile_patterns(lay: ProjLayout):
    per = lay.tn // V7X_LANES
    ntiles = lay.total // lay.tn
    pats = {}
    for j in range(ntiles):
        pats.setdefault(tuple(lay.chunk_types[j * per:(j + 1) * per]), []).append(j)
    out = []
    for pat, js in pats.items():
        ranges, start, prev = [], js[0], js[0]
        for j in js[1:]:
            if j != prev + 1:
                ranges.append((start, prev + 1))
                start = j
            prev = j
        ranges.append((start, prev + 1))
        out.append((pat, tuple(ranges)))
    return tuple(out)


def in_proj(x, g, sc, sh, w_pad, rope, cfg: Cfg, lay: ProjLayout, tm, rows_per_group):
    N, D = x.shape
    G, R, _ = sc.shape
    tiles_per_group = rows_per_group // tm
    assert N == G * rows_per_group and rows_per_group % tm == 0 and R in (1, tm)
    assert rope.shape[0] == rows_per_group
    kern = functools.partial(_inproj_kernel, patterns=_tile_patterns(lay), half_a=cfg.HD // 8, half_i=cfg.ID // 8)
    mod_spec = pl.BlockSpec((None, R, D), lambda i, j: (i // tiles_per_group, 0, 0))
    return pl.pallas_call(
        kern,
        grid=(N // tm, lay.total // lay.tn),
        in_specs=[pl.BlockSpec((tm, D), lambda i, j: (i, 0)),
                  pl.BlockSpec((1, D), lambda i, j: (0, 0)),
                  mod_spec, mod_spec,
                  pl.BlockSpec((D, lay.tn), lambda i, j: (0, j)),
                  pl.BlockSpec((tm, 6 * V7X_LANES), lambda i, j: (i % tiles_per_group, 0))],
        out_specs=pl.BlockSpec((tm, lay.tn), lambda i, j: (i, j)),
        out_shape=jax.ShapeDtypeStruct((N, lay.total), F32),
        scratch_shapes=[pltpu.VMEM((tm, D), BF16)],
        compiler_params=_cparams(("arbitrary", "arbitrary")),
        name="in_proj",
    )(x, g.reshape(1, D), sc, sh, w_pad, rope)


def _log_sigmoid(z):
    return jnp.minimum(z, 0.0) - jnp.log1p(jnp.exp(-jnp.abs(z)))


def _gla_prompt_kernel(q_ref, k_ref, v_ref, r_ref, a_ref, wa_ref, ba_ref, gn_ref, o_ref, s_ref, st_ref,
                       *, C, nchunk, rank, scale):
    t = pl.program_id(2)

    @pl.when(t == 0)
    def _():
        st_ref[...] = jnp.zeros_like(st_ref)

    row = lax.broadcasted_iota(I32, (C, C), 0)
    col = lax.broadcasted_iota(I32, (C, C), 1)
    causal = row >= col
    tri = jnp.where(causal, 1.0, 0.0).astype(BF16)
    wa = wa_ref[...]
    ba = ba_ref[...]
    gn = gn_ref[...]
    for c in range(nchunk):
        sl = slice(c * C, (c + 1) * C)
        q = q_ref[0, sl, :]
        k = k_ref[0, sl, :]
        v = v_ref[0, sl, :]
        a = a_ref[0, sl, :][:, :rank]
        la = _log_sigmoid(_bdot(a, wa) + ba) * (1.0 / GLA_TAU)
        hi, mid, lo = _split3(la)
        bc = (jnp.dot(tri, hi, preferred_element_type=F32) + jnp.dot(tri, mid, preferred_element_type=F32)
              + jnp.dot(tri, lo, preferred_element_type=F32))
        bend = bc[C - 1:C, :]
        qd = q * scale * jnp.exp(bc)
        kd = k * jnp.exp(-bc)
        ke = k * jnp.exp(bend - bc)
        att = jnp.where(causal, _dot_nt(qd, kd), 0.0)
        st = st_ref[...]
        o = _bdot(att, v) + _dot_nt(qd, st)
        st_ref[...] = st * jnp.exp(bend) + _dot_tn(v, ke)
        ms = jnp.mean(o * o, axis=-1, keepdims=True)
        on = o * lax.rsqrt(ms + NORM_EPS) * gn
        o_ref[0, sl, :] = on * _silu(r_ref[0, sl, :])

    @pl.when(t == pl.num_programs(2) - 1)
    def _():
        s_ref[0, 0] = st_ref[...].T


def gla_prompt(proj, w_gla_a, b_gla_a, g_gla_norm, cfg: Cfg, lay: ProjLayout):
    B, T, _ = proj.shape
    C = min(GLA_CHUNK, T)
    assert T % C == 0
    tc = C
    while tc * 2 <= 512 and T % (tc * 2) == 0:
        tc *= 2
    DK, DV, GH = cfg.DK, cfg.DV, cfg.GH
    oq, ok, ov, orr, oa = (lay.offs[n] for n in ("gq", "gk", "gv", "gr", "ga"))
    assert oq % DK == 0 and ok % DK == 0 and ov % DV == 0 and orr % DV == 0 and oa % V7X_LANES == 0
    kern = functools.partial(_gla_prompt_kernel, C=C, nchunk=tc // C, rank=cfg.RANK, scale=DK ** -0.5)
    return pl.pallas_call(
        kern,
        grid=(B, GH, T // tc),
        in_specs=[pl.BlockSpec((1, tc, DK), lambda b, h, t: (b, t, oq // DK + h)),
                  pl.BlockSpec((1, tc, DK), lambda b, h, t: (b, t, ok // DK + h)),
                  pl.BlockSpec((1, tc, DV), lambda b, h, t: (b, t, ov // DV + h)),
                  pl.BlockSpec((1, tc, DV), lambda b, h, t: (b, t, orr // DV + h)),
                  pl.BlockSpec((1, tc, V7X_LANES), lambda b, h, t: (b, t, oa // V7X_LANES)),
                  pl.BlockSpec((cfg.RANK, DK), lambda b, h, t: (0, h)),
                  pl.BlockSpec((1, DK), lambda b, h, t: (0, h)),
                  pl.BlockSpec((1, DV), lambda b, h, t: (0, 0))],
        out_specs=[pl.BlockSpec((1, tc, DV), lambda b, h, t: (b, t, h)),
                   pl.BlockSpec((1, 1, DK, DV), lambda b, h, t: (b, h, 0, 0))],
        out_shape=[jax.ShapeDtypeStruct((B, T, GH * DV), F32),
                   jax.ShapeDtypeStruct((B, GH, DK, DV), F32)],
        scratch_shapes=[pltpu.VMEM((DV, DK), F32)],
        compiler_params=_cparams(("arbitrary", "arbitrary", "arbitrary")),
        name="gla_prompt",
    )(proj, proj, proj, proj, proj, w_gla_a, b_gla_a.reshape(1, -1), g_gla_norm.reshape(1, DV))


def _gla_sample_kernel(q_ref, k_ref, v_ref, r_ref, a_ref, wa_ref, ba_ref, gn_ref, s0_ref, o_ref, s_ref,
                       *, GH, DK, DV, rank, scale):
    a = jnp.broadcast_to(a_ref[0][:, :rank], (V7X_SUBLANES, rank))
    ones = jnp.ones((V7X_SUBLANES, V7X_LANES), BF16)
    first = lax.broadcasted_iota(I32, (V7X_SUBLANES, 1), 0) == 0
    for h in range(GH):
        q = q_ref[0][:, h * DK:(h + 1) * DK]
        k = k_ref[0][:, h * DK:(h + 1) * DK]
        v = v_ref[0][:, h * DV:(h + 1) * DV]
        la = _log_sigmoid(_bdot(a, wa_ref[:, h * DK:(h + 1) * DK]) + ba_ref[:, h * DK:(h + 1) * DK])
        la = la[0:1, :] * (1.0 / GLA_TAU)
        dec = jnp.exp(la)
        qd = q * scale * dec
        kd = k * jnp.exp(-la)
        att = jnp.sum(qd * kd, axis=-1, keepdims=True)
        s0 = s0_ref[0, h]
        qd8 = jnp.broadcast_to(qd, (V7X_SUBLANES, DK))
        qh, ql = _split2(qd8)
        sh, sl = _split2(s0)
        qs = (jnp.dot(qh, sh, preferred_element_type=F32) + jnp.dot(qh, sl, preferred_element_type=F32)
              + jnp.dot(ql, sh, preferred_element_type=F32))[0:1, :]
        o = att * v + qs
        def col(x, width):
            x8 = jnp.where(first, jnp.broadcast_to(x, (V7X_SUBLANES, x.shape[1])), 0.0)
            parts = _split3(x8)
            rhs = ones if width == V7X_LANES else jnp.ones((V7X_SUBLANES, width), BF16)
            acc = None
            for p in parts:
                d = lax.dot_general(p, rhs, (((0,), (0,)), ((), ())), preferred_element_type=F32)
                acc = d if acc is None else acc + d
            return acc

        dec_col = col(dec, V7X_LANES)[:, 0:1]
        k8 = jnp.where(first, jnp.broadcast_to(k, (V7X_SUBLANES, DK)), 0.0)
        v8 = jnp.where(first, jnp.broadcast_to(v, (V7X_SUBLANES, DV)), 0.0)
        kh, kl = _split2(k8)
        vh, vl = _split2(v8)
        dn = (((0,), (0,)), ((), ()))
        outer = (lax.dot_general(kh, vh, dn, preferred_element_type=F32)
                 + lax.dot_general(kh, vl, dn, preferred_element_type=F32)
                 + lax.dot_general(kl, vh, dn, preferred_element_type=F32))
        s_ref[0, h] = s0 * dec_col + outer
        ms = jnp.mean(o * o, axis=-1, keepdims=True)
        on = o * lax.rsqrt(ms + NORM_EPS) * gn_ref[...]
        o_ref[0, :, h * DV:(h + 1) * DV] = on * _silu(r_ref[0][:, h * DV:(h + 1) * DV])


def gla_sample(proj, state, w_gla_a, b_gla_a, g_gla_norm, cfg: Cfg, lay: ProjLayout):
    DB = proj.shape[0]
    DK, DV, GH = cfg.DK, cfg.DV, cfg.GH
    qk, vv = GH * DK, GH * DV
    oq, ok, ov, orr, oa = (lay.offs[n] for n in ("gq", "gk", "gv", "gr", "ga"))
    assert oq % qk == 0 and ok % qk == 0 and ov % vv == 0 and orr % vv == 0
    kern = functools.partial(_gla_sample_kernel, GH=GH, DK=DK, DV=DV, rank=cfg.RANK, scale=DK ** -0.5)
    return pl.pallas_call(
        kern,
        grid=(DB,),
        in_specs=[pl.BlockSpec((1, 1, qk), lambda b: (b, 0, oq // qk)),
                  pl.BlockSpec((1, 1, qk), lambda b: (b, 0, ok // qk)),
                  pl.BlockSpec((1, 1, vv), lambda b: (b, 0, ov // vv)),
                  pl.BlockSpec((1, 1, vv), lambda b: (b, 0, orr // vv)),
                  pl.BlockSpec((1, 1, V7X_LANES), lambda b: (b, 0, oa // V7X_LANES)),
                  pl.BlockSpec((cfg.RANK, qk), lambda b: (0, 0)),
                  pl.BlockSpec((1, qk), lambda b: (0, 0)),
                  pl.BlockSpec((1, DV), lambda b: (0, 0)),
                  pl.BlockSpec((1, GH, DK, DV), lambda b: (b, 0, 0, 0))],
        out_specs=[pl.BlockSpec((1, 1, vv), lambda b: (b, 0, 0)),
                   pl.BlockSpec((1, GH, DK, DV), lambda b: (b, 0, 0, 0))],
        out_shape=[jax.ShapeDtypeStruct((DB, 1, vv), F32),
                   jax.ShapeDtypeStruct((DB, GH, DK, DV), F32)],
        compiler_params=_cparams(("arbitrary",)),
        name="gla_sample",
    )(proj, proj, proj, proj, proj, w_gla_a, b_gla_a.reshape(1, -1), g_gla_norm.reshape(1, DV), state)


def _offset_word_to_f32(t):
    key = t ^ INT_MIN
    return pltpu.bitcast(key ^ ((key >> 31) & INT_MAX), F32)


def _dsa_prompt_kernel(iq_ref, iw_ref, aq_ref, kidx_ref, kbf_ref, vbf_ref, o_ref,
                       sc_ref, lg_ref, q2_ref, wb_ref, *, IH, ID, NH, G, HD, topk, wscale, sm_scale, KEY_CHUNK):
    qb = pl.program_id(1)
    QB = Q_BLOCK
    L = V7X_LANES
    CH = KEY_CHUNK // L
    nch = qb // CH + 1
    lane = lax.broadcasted_iota(I32, (QB, L), 1)
    rowq = lax.broadcasted_iota(I32, (QB, 1), 0) + qb * QB
    first = lane < ID

    for h in range(IH):
        pair = iq_ref[0][:, (h // 2) * L:(h // 2 + 1) * L]
        qq = jnp.where((lane // ID) == (h % 2), pair, pltpu.roll(pair, ID, 1))
        hi, lo = _split2(qq)
        q2_ref[h * QB:(h + 1) * QB, 0:L] = jnp.where(first, hi, lo)
        q2_ref[h * QB:(h + 1) * QB, L:2 * L] = jnp.where(first, hi, jnp.zeros_like(hi))
        wb_ref[h * QB:(h + 1) * QB, :] = jnp.broadcast_to(iw_ref[0][:, h:h + 1] * wscale, (QB, 2 * L))

    lane2 = lax.broadcasted_iota(I32, (QB, 2 * L), 1)
    first_k = lax.broadcasted_iota(I32, (2 * L, L), 1) < ID

    def score_body(c, carry):
        for half in range(CH // 2):
            k0 = c * KEY_CHUNK + half * 2 * L
            cols = slice(half * 2 * L, (half + 1) * 2 * L)

            @pl.when(k0 < (qb + 1) * QB)
            def _():
                kblk = kidx_ref[0, pl.ds(pl.multiple_of(k0, 2 * L), 2 * L), :]
                khi, klo = _split2(kblk)
                k2 = jnp.concatenate([khi, jnp.where(first_k, klo, jnp.zeros_like(klo))], axis=1)
                s = lax.dot_general(q2_ref[...], k2, (((1,), (1,)), ((), ())),
                                    preferred_element_type=F32)
                s = jnp.maximum(s, 0.0) * wb_ref[...]
                sc = s[0:QB]
                for h in range(1, IH):
                    sc = sc + s[h * QB:(h + 1) * QB]
                sc_ref[c, :, cols] = jnp.where(k0 + lane2 <= rowq, sc, -jnp.inf)

            @pl.when(k0 >= (qb + 1) * QB)
            def _():
                sc_ref[c, :, cols] = jnp.full((QB, 2 * L), -jnp.inf, F32)
        return carry

    lax.fori_loop(0, nch, score_body, 0)

    def sweep(fn, init, combine):
        def body(c, acc):
            slab = sc_ref[c]
            for j in range(CH):
                acc = combine(acc, fn(slab[:, j * L:(j + 1) * L], c * KEY_CHUNK + j * L + lane))
            return acc
        return lax.fori_loop(0, nch, body, init)

    def count(pred):
        acc = sweep(lambda s, pos: jnp.where(pred(s, pos), 1, 0), jnp.zeros((QB, L), I32), jnp.add)
        return jnp.sum(acc, axis=1, keepdims=True)

    want = jnp.minimum(rowq + 1, topk)

    def bit_body(i, t):
        cand = t | (jnp.int32(1) << (31 - i))
        cf = _offset_word_to_f32(cand)
        cnt = count(lambda s, pos: s >= cf)
        return jnp.where(cnt >= want, cand, t)

    thr0 = _offset_word_to_f32(lax.fori_loop(0, 32, bit_body, jnp.zeros((QB, 1), I32)))
    n_ge = count(lambda s, pos: s >= thr0)
    thr = jnp.min(sweep(lambda s, pos: jnp.where(s >= thr0, s, jnp.inf), jnp.full((QB, L), jnp.inf, F32),
                        jnp.minimum), axis=1, keepdims=True)
    cut = jnp.full((QB, 1), INT_MAX, I32)

    def selected(s, pos, thr, cut):
        return (s > thr) | ((s == thr) & (pos <= cut))

    def drop_last(_, tc):
        thr, cut = tc
        over = count(lambda s, pos: selected(s, pos, thr, cut)) > want
        low = jnp.min(sweep(lambda s, pos: jnp.where(selected(s, pos, thr, cut), s, jnp.inf),
                            jnp.full((QB, L), jnp.inf, F32), jnp.minimum), axis=1, keepdims=True)
        last = jnp.max(sweep(lambda s, pos: jnp.where(selected(s, pos, thr, cut) & (s == low), pos, -1),
                             jnp.full((QB, L), -1, I32), jnp.maximum), axis=1, keepdims=True)
        return jnp.where(over, low, thr), jnp.where(over, last - 1, cut)

    thr, cut = lax.fori_loop(0, jnp.max(n_ge - want), drop_last, (thr, cut))

    def mask_body(c, carry):
        slab = sc_ref[c]
        pos = c * KEY_CHUNK + lax.broadcasted_iota(I32, slab.shape, 1)
        sc_ref[c] = jnp.where(selected(slab, pos, thr, cut), 0.0, -jnp.inf)
        return carry

    lax.fori_loop(0, nch, mask_body, 0)

    c2 = sm_scale * math.log2(math.e)
    for g in range(NH // G):
        qg = jnp.concatenate([aq_ref[0][:, (g * G + j) * HD:(g * G + j + 1) * HD] for j in range(G)],
                             axis=0).astype(BF16)

        def logit_body(c, mx):
            rows = pl.ds(pl.multiple_of(c * KEY_CHUNK, KEY_CHUNK), KEY_CHUNK)
            s = lax.dot_general(qg, kbf_ref[0, rows, g * HD:(g + 1) * HD], (((1,), (1,)), ((), ())),
                                preferred_element_type=F32) * c2
            mask = sc_ref[c]
            for j in range(G):
                sj = s[j * QB:(j + 1) * QB] + mask
                lg_ref[c, j * QB:(j + 1) * QB, :] = sj
                for b in range(CH):
                    mx[j] = jnp.maximum(mx[j], sj[:, b * L:(b + 1) * L])
            return mx

        mx = lax.fori_loop(0, nch, logit_body, [jnp.full((QB, L), NEG_BIG, F32) for _ in range(G)])
        m = jnp.concatenate([jnp.broadcast_to(jnp.max(x, axis=1, keepdims=True), (QB, L)) for x in mx],
                            axis=0)

        def value_body(c, carry):
            ls, acc = carry
            rows = pl.ds(pl.multiple_of(c * KEY_CHUNK, KEY_CHUNK), KEY_CHUNK)
            ps = [jnp.exp2(lg_ref[c, :, b * L:(b + 1) * L] - m) for b in range(CH)]
            for p in ps:
                ls = ls + p
            p = jnp.concatenate(ps, axis=1).astype(BF16)
            acc = acc + jnp.dot(p, vbf_ref[0, rows, g * HD:(g + 1) * HD], preferred_element_type=F32)
            return ls, acc

        ls, acc = lax.fori_loop(0, nch, value_body, (jnp.zeros((G * QB, L), F32), jnp.zeros((G * QB, HD), F32)))
        out = acc / jnp.sum(ls, axis=1, keepdims=True)
        for j in range(G):
            o_ref[0, :, (g * G + j) * HD:(g * G + j + 1) * HD] = out[j * QB:(j + 1) * QB]


def dsa_prompt(proj, kbf, vbf, cfg: Cfg, lay: ProjLayout):
    B, T, _ = proj.shape
    QB = Q_BLOCK
    chunk = min(KEY_CHUNK, T)
    assert T % chunk == 0 and chunk % (2 * V7X_LANES) == 0 and cfg.ID * 2 == V7X_LANES and cfg.IH % 2 == 0
    topk = min(TOPK_MAX, T // 4)
    iqw, aqw, kvw = cfg.IH * cfg.ID, cfg.NH * cfg.HD, cfg.NKV * cfg.HD
    oiq, oaq, oik, oiw = (lay.offs[n] for n in ("iq", "aq", "ik", "iw"))
    assert oiq % iqw == 0 and oaq % aqw == 0
    G = cfg.NH // cfg.NKV
    kern = functools.partial(_dsa_prompt_kernel, IH=cfg.IH, ID=cfg.ID, NH=cfg.NH, G=G, HD=cfg.HD, topk=topk,
                             wscale=(cfg.IH ** -0.5) * (cfg.ID ** -0.5), sm_scale=cfg.HD ** -0.5, KEY_CHUNK=chunk)
    return pl.pallas_call(
        kern,
        grid=(B, T // QB),
        in_specs=[pl.BlockSpec((1, QB, iqw), lambda b, q: (b, q, oiq // iqw)),
                  pl.BlockSpec((1, QB, V7X_LANES), lambda b, q: (b, q, oiw // V7X_LANES)),
                  pl.BlockSpec((1, QB, aqw), lambda b, q: (b, q, oaq // aqw)),
                  pl.BlockSpec((1, T, V7X_LANES), lambda b, q: (b, 0, oik // V7X_LANES)),
                  pl.BlockSpec((1, T, kvw), lambda b, q: (b, 0, 0)),
                  pl.BlockSpec((1, T, kvw), lambda b, q: (b, 0, 0))],
        out_specs=pl.BlockSpec((1, QB, aqw), lambda b, q: (b, q, 0)),
        out_shape=jax.ShapeDtypeStruct((B, T, aqw), F32),
        scratch_shapes=[pltpu.VMEM((T // chunk, QB, chunk), F32),
                        pltpu.VMEM((T // chunk, G * QB, chunk), F32),
                        pltpu.VMEM((cfg.IH * QB, 2 * V7X_LANES), BF16),
                        pltpu.VMEM((cfg.IH * QB, 2 * V7X_LANES), F32)],
        compiler_params=_cparams(("arbitrary", "arbitrary")),
        name="dsa_prompt",
    )(proj, proj, proj, proj, kbf, vbf)


def _dsa_sample_select_kernel(pt_ref, q_ref, w_ref, kn_ref, kidx_hbm, idx_ref, kbuf, sc_ref, rank_ref, sel_ref, sem,
                              *, NP, NPP, PG, topk, wscale):
    b = pl.program_id(0)
    nb = pl.num_programs(0)
    slot = b % 2
    L = V7X_LANES

    def page_copy(sl, p, page):
        return pltpu.make_async_copy(kidx_hbm.at[0, page], kbuf.at[sl, p], sem.at[sl])

    def issue(bb, sl):
        def body(p, c):
            page_copy(sl, p, pt_ref[bb, p]).start()
            return c
        lax.fori_loop(0, NP, body, 0, unroll=math.gcd(NP, 2 * DMA_LOOP_UNROLL))

    @pl.when(b == 0)
    def _():
        issue(0, 0)

    @pl.when(b + 1 < nb)
    def _():
        issue(b + 1, 1 - slot)

    def wait_body(p, c):
        page_copy(slot, p, 0).wait()
        return c

    lax.fori_loop(0, NP, wait_body, 0, unroll=math.gcd(NP, 2 * DMA_LOOP_UNROLL))

    q = q_ref[0]
    w = w_ref[0][:, 0:1] * wscale
    qh, ql = _split2(q)
    q3 = jnp.concatenate([qh, ql, qh], axis=1)
    dn = (((1,), (1,)), ((), ()))

    def group_body(gi, c):
        kk = kbuf[slot, pl.ds(gi * PG, PG)]
        kk = jnp.concatenate([kk[j] for j in range(PG)], axis=1)
        kh, kl = _split2(kk)
        s = jnp.dot(q3, jnp.concatenate([kh, kh, kl], axis=0), preferred_element_type=F32)
        sc = jnp.sum(jnp.maximum(s, 0.0) * w, axis=0, keepdims=True)
        for j in range(PG):
            sc_ref[pl.ds(gi * PG + j, 1), :] = sc[:, j * L:(j + 1) * L]
        return c

    lax.fori_loop(0, NP // PG, group_body, 0)
    sn = jnp.sum(q * kn_ref[0], axis=1, keepdims=True)
    scn = jnp.sum(jnp.maximum(sn, 0.0) * w, axis=0, keepdims=True)
    tail = (NPP - NP, L)
    first = (lax.broadcasted_iota(I32, tail, 0) == 0) & (lax.broadcasted_iota(I32, tail, 1) == 0)
    sc_ref[NP:NPP, :] = jnp.where(first, scn, -jnp.inf)

    sc = sc_ref[...]
    pos = lax.broadcasted_iota(I32, sc.shape, 0) * L + lax.broadcasted_iota(I32, sc.shape, 1)

    def bit_body(i, t):
        cand = t | (jnp.int32(1) << (31 - i))
        hit = jnp.where(sc >= _offset_word_to_f32(cand), 1, 0)
        cnt = jnp.sum(jnp.sum(hit, axis=0, keepdims=True), axis=1, keepdims=True)
        return jnp.where(cnt >= topk, cand, t)

    thr0 = _offset_word_to_f32(lax.fori_loop(0, 32, bit_body, jnp.zeros((1, L), I32)))
    n_ge = jnp.sum(jnp.where(sc >= thr0, 1, 0))
    thr = jnp.min(jnp.where(sc >= thr0, sc, jnp.inf))
    cut = jnp.int32(INT_MAX)

    def selected(thr, cut):
        return (sc > thr) | ((sc == thr) & (pos <= cut))

    def drop_last(_, tc):
        thr, cut = tc
        sel = selected(thr, cut)
        low = jnp.min(jnp.where(sel, sc, jnp.inf))
        last = jnp.max(jnp.where(sel & (sc == low), pos, -1))
        return low, last - 1

    thr, cut = lax.fori_loop(0, n_ge - topk, drop_last, (thr, cut))
    sel = selected(thr, cut)

    self_ = jnp.where(sel, 1.0, 0.0)
    li = lax.broadcasted_iota(I32, (L, L), 0)
    lj = lax.broadcasted_iota(I32, (L, L), 1)
    before_lane = jnp.where(li < lj, 1.0, 0.0).astype(BF16)
    ri = lax.broadcasted_iota(I32, (NPP, NPP), 0)
    rj = lax.broadcasted_iota(I32, (NPP, NPP), 1)
    before_row = jnp.where(rj < ri, 1.0, 0.0).astype(BF16)
    rowtot = jnp.broadcast_to(jnp.sum(self_, axis=1, keepdims=True), (NPP, L))
    rank_ref[...] = (jnp.dot(self_.astype(BF16), before_lane, preferred_element_type=F32)
                     + jnp.dot(before_row, rowtot.astype(BF16), preferred_element_type=F32))
    sel_ref[...] = self_

    slot_id = lax.broadcasted_iota(I32, (topk, 1), 0).astype(F32)
    lane1 = lax.broadcasted_iota(I32, (1, L), 1)

    def place(r, acc):
        hit = (rank_ref[pl.ds(r, 1), :] == slot_id) & (sel_ref[pl.ds(r, 1), :] > 0.0)
        return acc + jnp.where(hit, (r * L + lane1).astype(F32), 0.0)

    placed = lax.fori_loop(0, NPP, place, jnp.zeros((topk, L), F32)).astype(I32)
    ones = jnp.ones((V7X_SUBLANES, L), BF16)
    lane_bits = L.bit_length() - 1
    hi = (placed >> lane_bits).astype(F32).astype(BF16)
    lo = (placed & (L - 1)).astype(F32).astype(BF16)
    idx = (lax.dot_general(ones, hi, dn, preferred_element_type=F32) * float(L)
           + lax.dot_general(ones, lo, dn, preferred_element_type=F32))
    idx_ref[0] = idx.astype(I32)


def _dsa_sample_attend_kernel(idx_s, pt_s, q_ref, idx_ref, kn_ref, vn_ref, ck_hbm, cv_hbm, o_ref, kbuf, vbuf, sem,
                              *, NKV, G, topk, past, page_shift, sm_scale):
    b = pl.program_id(0)
    nb = pl.num_programs(0)
    slot = b % 2
    page_mask = (1 << page_shift) - 1

    def row_copies(sl, j, page, r):
        return (pltpu.make_async_copy(ck_hbm.at[0, page, r], kbuf.at[sl, j], sem.at[0, sl]),
                pltpu.make_async_copy(cv_hbm.at[0, page, r], vbuf.at[sl, j], sem.at[1, sl]))

    def issue(bb, sl):
        def body(j, c):
            i = jnp.minimum(idx_s[bb, j], past - 1)
            for cp in row_copies(sl, j, pt_s[bb, i >> page_shift], i & page_mask):
                cp.start()
            return c
        lax.fori_loop(0, topk, body, 0, unroll=2 * DMA_LOOP_UNROLL)

    @pl.when(b == 0)
    def _():
        issue(0, 0)

    @pl.when(b + 1 < nb)
    def _():
        issue(b + 1, 1 - slot)

    def wait_body(j, c):
        for cp in row_copies(slot, j, 0, 0):
            cp.wait()
        return c

    lax.fori_loop(0, topk, wait_body, 0, unroll=2 * DMA_LOOP_UNROLL)

    idx = idx_ref[0][0:1, :]
    cached = idx < past
    new_sel = jnp.max(jnp.where(idx == past, 1, 0), axis=1, keepdims=True) > 0
    for g in range(NKV):
        qg = q_ref[0][g * G:(g + 1) * G, :]
        kg = kbuf[slot, :, g, :]
        vg = vbuf[slot, :, g, :]
        s = jnp.where(cached, _dot_nt(qg, kg) * sm_scale, -jnp.inf)
        sn = jnp.sum(qg * kn_ref[0][g:g + 1, :], axis=1, keepdims=True) * sm_scale
        sn = jnp.where(new_sel, sn, -jnp.inf)
        m = jnp.maximum(jnp.max(s, axis=1, keepdims=True), sn)
        p = jnp.exp(s - m)
        pn = jnp.exp(sn - m)
        l = jnp.sum(p, axis=1, keepdims=True) + pn
        o_ref[0, g * G:(g + 1) * G, :] = (_bdot(p, vg) + pn * vn_ref[0][g:g + 1, :]) / l


def dsa_sample(q_att, q_idx, w_idx, kn_idx, k_new, v_new, cache_k, cache_v, cache_kidx, page_table, cfg: Cfg):
    DB, NP, PAGE = cfg.DB, cfg.NPAGES, cfg.PAGE
    assert PAGE == V7X_LANES and cache_k.shape[0] == 1
    total = cfg.PAST + 1
    topk = min(TOPK_MAX, total // 4)
    assert topk <= total and topk % V7X_LANES == 0
    NPP = -(-(NP + 1) // V7X_SUBLANES) * V7X_SUBLANES
    PG = math.gcd(NP, 8)
    w_b = jnp.broadcast_to(w_idx[:, :, None], (DB, cfg.IH, V7X_LANES))
    sel_kern = functools.partial(_dsa_sample_select_kernel, NP=NP, NPP=NPP, PG=PG, topk=topk,
                                 wscale=(cfg.IH ** -0.5) * (cfg.ID ** -0.5))
    idx = pl.pallas_call(
        sel_kern,
        grid_spec=pltpu.PrefetchScalarGridSpec(
            num_scalar_prefetch=1,
            grid=(DB,),
            in_specs=[pl.BlockSpec((1, cfg.IH, cfg.ID), lambda b, pt: (b, 0, 0)),
                      pl.BlockSpec((1, cfg.IH, V7X_LANES), lambda b, pt: (b, 0, 0)),
                      pl.BlockSpec((1, 1, cfg.ID), lambda b, pt: (b, 0, 0)),
                      pl.BlockSpec(memory_space=pl.ANY)],
            out_specs=pl.BlockSpec((1, V7X_SUBLANES, topk), lambda b, pt: (b, 0, 0)),
            scratch_shapes=[pltpu.VMEM((2, NP, cfg.ID, PAGE), F32),
                            pltpu.VMEM((NPP, V7X_LANES), F32),
                            pltpu.VMEM((NPP, V7X_LANES), F32),
                            pltpu.VMEM((NPP, V7X_LANES), F32),
                            pltpu.SemaphoreType.DMA((2,))]),
        out_shape=jax.ShapeDtypeStruct((DB, V7X_SUBLANES, topk), I32),
        compiler_params=_cparams(("arbitrary",)),
        name="dsa_sample_select",
    )(page_table, q_idx, w_b, kn_idx.reshape(DB, 1, cfg.ID), jnp.swapaxes(cache_kidx, 2, 3))

    G = cfg.NH // cfg.NKV
    att_kern = functools.partial(_dsa_sample_attend_kernel, NKV=cfg.NKV, G=G, topk=topk, past=cfg.PAST,
                                 page_shift=PAGE.bit_length() - 1, sm_scale=cfg.HD ** -0.5)
    return pl.pallas_call(
        att_kern,
        grid_spec=pltpu.PrefetchScalarGridSpec(
            num_scalar_prefetch=2,
            grid=(DB,),
            in_specs=[pl.BlockSpec((1, cfg.NH, cfg.HD), lambda b, ix, pt: (b, 0, 0)),
                      pl.BlockSpec((1, V7X_SUBLANES, topk), lambda b, ix, pt: (b, 0, 0)),
                      pl.BlockSpec((1, cfg.NKV, cfg.HD), lambda b, ix, pt: (b, 0, 0)),
                      pl.BlockSpec((1, cfg.NKV, cfg.HD), lambda b, ix, pt: (b, 0, 0)),
                      pl.BlockSpec(memory_space=pl.ANY),
                      pl.BlockSpec(memory_space=pl.ANY)],
            out_specs=pl.BlockSpec((1, cfg.NH, cfg.HD), lambda b, ix, pt: (b, 0, 0)),
            scratch_shapes=[pltpu.VMEM((2, topk, cfg.NKV, cfg.HD), F32),
                            pltpu.VMEM((2, topk, cfg.NKV, cfg.HD), F32),
                            pltpu.SemaphoreType.DMA((2, 2))]),
        out_shape=jax.ShapeDtypeStruct((DB, cfg.NH, cfg.HD), F32),
        compiler_params=_cparams(("arbitrary",)),
        name="dsa_sample_attend",
    )(idx[:, 0, :], page_table, q_att, idx, k_new.reshape(DB, cfg.NKV, cfg.HD), v_new.reshape(DB, cfg.NKV, cfg.HD),
      cache_k, cache_v)


def _merge_kernel(og_ref, oa_ref, wg_ref, wa_ref, za_ref, zb_ref, o_ref, ogb_ref, oab_ref):
    @pl.when(pl.program_id(1) == 0)
    def _():
        ogb_ref[...] = og_ref[...].astype(BF16)
        oab_ref[...] = oa_ref[...].astype(BF16)

    a = jnp.dot(ogb_ref[...], wg_ref[...].astype(BF16), preferred_element_type=F32)
    b = jnp.dot(oab_ref[...], wa_ref[...].astype(BF16), preferred_element_type=F32)
    o_ref[...] = _sigmoid(za_ref[...]) * a + _sigmoid(zb_ref[...]) * b


def merge_branches(og, oa, w_gla_o, w_attn_o, proj, cfg: Cfg, lay: ProjLayout, tm):
    N = og.shape[0]
    D, tn = cfg.D, lay.tn
    oza, ozb = lay.offs["za"] // tn, lay.offs["zb"] // tn
    return pl.pallas_call(
        _merge_kernel,
        grid=(N // tm, D // tn),
        in_specs=[pl.BlockSpec((tm, og.shape[1]), lambda i, j: (i, 0)),
                  pl.BlockSpec((tm, oa.shape[1]), lambda i, j: (i, 0)),
                  pl.BlockSpec((og.shape[1], tn), lambda i, j: (0, j)),
                  pl.BlockSpec((oa.shape[1], tn), lambda i, j: (0, j)),
                  pl.BlockSpec((tm, tn), lambda i, j: (i, oza + j)),
                  pl.BlockSpec((tm, tn), lambda i, j: (i, ozb + j))],
        out_specs=pl.BlockSpec((tm, tn), lambda i, j: (i, j)),
        out_shape=jax.ShapeDtypeStruct((N, D), F32),
        scratch_shapes=[pltpu.VMEM((tm, og.shape[1]), BF16), pltpu.VMEM((tm, oa.shape[1]), BF16)],
        compiler_params=_cparams(("arbitrary", "arbitrary")),
        name="merge_branches",
    )(og, oa, w_gla_o, w_attn_o, proj, proj)


def _outproj_kernel(m_ref, w_ref, x_ref, gt_ref, o_ref, mb_ref):
    @pl.when(pl.program_id(1) == 0)
    def _():
        mb_ref[...] = m_ref[...].astype(BF16)

    o_ref[...] = x_ref[...] + gt_ref[...] * jnp.dot(mb_ref[...], w_ref[...].astype(BF16), preferred_element_type=F32)


def out_proj(merged, w_out, x, gt, tm, rows_per_group, tn):
    N, D = x.shape
    G, R, _ = gt.shape
    tpg = rows_per_group // tm
    return pl.pallas_call(
        _outproj_kernel,
        grid=(N // tm, D // tn),
        in_specs=[pl.BlockSpec((tm, D), lambda i, j: (i, 0)),
                  pl.BlockSpec((D, tn), lambda i, j: (0, j)),
                  pl.BlockSpec((tm, tn), lambda i, j: (i, j)),
                  pl.BlockSpec((None, R, tn), lambda i, j: (i // tpg, 0, j))],
        out_specs=pl.BlockSpec((tm, tn), lambda i, j: (i, j)),
        out_shape=jax.ShapeDtypeStruct((N, D), F32),
        scratch_shapes=[pltpu.VMEM((tm, D), BF16)],
        compiler_params=_cparams(("arbitrary", "arbitrary")),
        name="out_proj",
    )(merged, w_out, x, gt)


def _ffn_pre_kernel(x_ref, g_ref, sc_ref, sh_ref, wrt_ref, br_ref, wg_ref, wu_ref, wd_ref,
                    h2_ref, ysh_ref, eidx_ref, wts_ref,
                    wrh_ref, wrl_ref, wgb_ref, wub_ref, wdb_ref, *, E, NG, first_tile):
    @pl.when(pl.program_id(0) == first_tile)
    def _():
        hi, lo = _split2(wrt_ref[...])
        wrh_ref[...] = hi
        wrl_ref[...] = lo
        wgb_ref[...] = wg_ref[...].astype(BF16)
        wub_ref[...] = wu_ref[...].astype(BF16)
        wdb_ref[...] = wd_ref[...].astype(BF16)

    x = x_ref[...]
    tm = x.shape[0]
    ms = jnp.mean(x * x, axis=-1, keepdims=True)
    h2 = x * lax.rsqrt(ms + NORM_EPS) * g_ref[...] * (1.0 + sc_ref[...]) + sh_ref[...]
    h2_ref[...] = h2
    hh, hl = _split2(h2)

    gate = jnp.dot(hh, wgb_ref[...], preferred_element_type=F32)
    up = jnp.dot(hh, wub_ref[...], preferred_element_type=F32)
    ysh_ref[...] = jnp.dot((_silu(gate) * up).astype(BF16), wdb_ref[...], preferred_element_type=F32)

    dn = (((1,), (1,)), ((), ()))
    logit = (lax.dot_general(wrh_ref[...], hh, dn, preferred_element_type=F32)
             + lax.dot_general(wrh_ref[...], hl, dn, preferred_element_type=F32)
             + lax.dot_general(wrl_ref[...], hh, dn, preferred_element_type=F32))
    s = _sigmoid(logit)
    PG = E // NG
    s3 = s.reshape(NG, PG, tm)
    sel3 = (s + br_ref[:, 0:1]).reshape(NG, PG, tm)
    ninf = -jnp.inf
    ipg = lax.broadcasted_iota(I32, (NG, PG, tm), 1)
    m1 = jnp.max(sel3, axis=1, keepdims=True)
    i1 = jnp.min(jnp.where(sel3 == m1, ipg, PG), axis=1, keepdims=True)
    m2 = jnp.max(jnp.where(ipg == i1, ninf, sel3), axis=1, keepdims=True)
    gs = (m1 + m2).reshape(NG, tm)
    ig = lax.broadcasted_iota(I32, (NG, tm), 0)
    gmask = jnp.zeros((NG, tm), jnp.bool_)
    for _ in range(TOPK_GROUPS):
        m = jnp.max(gs, axis=0, keepdims=True)
        i = jnp.min(jnp.where(gs == m, ig, NG), axis=0, keepdims=True)
        pick = ig == i
        gmask = gmask | pick
        gs = jnp.where(pick, ninf, gs)
    cand = jnp.where(gmask.reshape(NG, 1, tm), sel3, ninf)
    ie = lax.broadcasted_iota(I32, (NG, PG, tm), 0) * PG + ipg
    idxs, ws = [], []
    for _ in range(TOP_K):
        m = jnp.max(jnp.max(cand, axis=0, keepdims=True), axis=1, keepdims=True)
        hit = jnp.where(cand == m, ie, E)
        i = jnp.min(jnp.min(hit, axis=0, keepdims=True), axis=1, keepdims=True)
        pick = ie == i
        wsel = jnp.where(pick, s3, 0.0)
        ws.append(jnp.sum(jnp.sum(wsel, axis=0, keepdims=True), axis=1, keepdims=True).reshape(1, tm))
        idxs.append(i.reshape(1, tm))
        cand = jnp.where(pick, ninf, cand)
    tot = ws[0]
    for w in ws[1:]:
        tot = tot + w
    for k in range(TOP_K):
        eidx_ref[k:k + 1, :] = idxs[k]
        wts_ref[k:k + 1, :] = ws[k] / tot * ROUTED_SCALE


def ffn_pre(x1, g, sc, sh, w_router_t, b_router, w_sh_gate, w_sh_up, w_sh_down, cfg: Cfg, tm):
    N, D = x1.shape
    G, R, _ = sc.shape
    tpg = (N // G) // tm
    E, DE = cfg.E, cfg.DE
    assert E // N_GROUPS == V7X_SUBLANES
    mod_spec = pl.BlockSpec((None, R, D), lambda i: (i // tpg, 0, 0))
    const = lambda shape: pl.BlockSpec(shape, lambda i: (0,) * len(shape))
    return pl.pallas_call(
        functools.partial(_ffn_pre_kernel, E=E, NG=N_GROUPS, first_tile=0),
        grid=(N // tm,),
        in_specs=[pl.BlockSpec((tm, D), lambda i: (i, 0)), const((1, D)), mod_spec, mod_spec,
                  const((E, D)), const((E, V7X_LANES)), const((D, DE)), const((D, DE)), const((DE, D))],
        out_specs=[pl.BlockSpec((tm, D), lambda i: (i, 0)),
                   pl.BlockSpec((tm, D), lambda i: (i, 0)),
                   pl.BlockSpec((TOP_K, tm), lambda i: (0, i)),
                   pl.BlockSpec((TOP_K, tm), lambda i: (0, i))],
        out_shape=[jax.ShapeDtypeStruct((N, D), F32),
                   jax.ShapeDtypeStruct((N, D), F32),
                   jax.ShapeDtypeStruct((TOP_K, N), I32),
                   jax.ShapeDtypeStruct((TOP_K, N), F32)],
        scratch_shapes=[pltpu.VMEM((E, D), BF16), pltpu.VMEM((E, D), BF16),
                        pltpu.VMEM((D, DE), BF16), pltpu.VMEM((D, DE), BF16), pltpu.VMEM((DE, D), BF16)],
        compiler_params=_cparams(("arbitrary",)),
        name="ffn_pre",
    )(x1, g.reshape(1, D), sc, sh, w_router_t, jnp.broadcast_to(b_router[:, None], (E, V7X_LANES)),
      w_sh_gate, w_sh_up, w_sh_down)


def _rank_kernel(eidx_ref, rank_ref, cnt_ref, carry_ref, *, E):
    i = pl.program_id(0)

    @pl.when(i == 0)
    def _():
        carry_ref[...] = jnp.zeros(carry_ref.shape, F32)

    eidx = eidx_ref[...]
    tm = eidx.shape[1]
    eio = lax.broadcasted_iota(I32, (E, 1), 0)
    onehot = jnp.zeros((E, tm), F32)
    for k in range(TOP_K):
        onehot = onehot + jnp.where(eidx[k:k + 1, :] == eio, 1.0, 0.0)
    r = lax.broadcasted_iota(I32, (tm, tm), 0)
    c = lax.broadcasted_iota(I32, (tm, tm), 1)
    tri = jnp.where(r <= c, 1.0, 0.0).astype(BF16)
    cum = jnp.dot(onehot.astype(BF16), tri, preferred_element_type=F32)
    excl = cum - onehot + carry_ref[:, 0:1]
    for k in range(TOP_K):
        hit = eidx[k:k + 1, :] == eio
        rank_ref[k:k + 1, :] = jnp.sum(jnp.where(hit, excl, 0.0), axis=0, keepdims=True).astype(I32)
    carry_ref[...] = carry_ref[...] + cum[:, tm - 1:tm]

    @pl.when(i == pl.num_programs(0) - 1)
    def _():
        cnt_ref[...] = carry_ref[...].astype(I32)


def _plan_kernel(eidx_ref, rank_ref, cnt_ref, dest_ref, be_ref, nr_ref, *, E, bm, NBP):
    L = V7X_LANES
    sh = bm.bit_length() - 1
    cnt = cnt_ref[...]
    padded = ((cnt + (bm - 1)) >> sh) << sh
    r = lax.broadcasted_iota(I32, (E, E), 0)
    c = lax.broadcasted_iota(I32, (E, E), 1)
    low = jnp.where(c < r, 1.0, 0.0).astype(BF16)
    pstart = jnp.zeros((E, L), F32)
    for shift in (0, 8, 16):
        piece = ((padded >> shift) & 255).astype(F32).astype(BF16)
        pstart = pstart + jnp.dot(low, piece, preferred_element_type=F32) * float(1 << shift)
    pstart = pstart.astype(I32)
    pend = pstart + padded
    eio = lax.broadcasted_iota(I32, (E, 1), 0)
    eidx = eidx_ref[...]
    ps_col = pstart[:, 0:1]
    for k in range(TOP_K):
        e = eidx[k:k + 1, :]
        base = jnp.sum(jnp.where(e == eio, ps_col, 0), axis=0, keepdims=True)
        dest_ref[k:k + 1, :] = jnp.where(e >= 0, base + rank_ref[k:k + 1, :], -1)

    @pl.when(pl.program_id(0) == 0)
    def _():
        blk0 = lax.broadcasted_iota(I32, (1, NBP), 1) * bm
        be = jnp.sum((pend[:, 0:1] <= blk0).astype(I32), axis=0, keepdims=True)
        be = jnp.minimum(be, E - 1)
        last = jnp.sum(jnp.where(eio == be, (pstart + cnt)[:, 0:1], 0), axis=0, keepdims=True)
        be_ref[...] = be
        nr_ref[...] = jnp.clip(last - blk0, 0, bm)


def moe_plan(eidx, cfg: Cfg, bm, tm):
    K, N = eidx.shape
    E = cfg.E
    NB = -(-(N * K) // bm) + E
    NBP = -(-NB // V7X_LANES) * V7X_LANES
    rank, cnt = pl.pallas_call(
        functools.partial(_rank_kernel, E=E),
        grid=(N // tm,),
        in_specs=[pl.BlockSpec((K, tm), lambda i: (0, i))],
        out_specs=[pl.BlockSpec((K, tm), lambda i: (0, i)), pl.BlockSpec((E, V7X_LANES), lambda i: (0, 0))],
        out_shape=[jax.ShapeDtypeStruct((K, N), I32), jax.ShapeDtypeStruct((E, V7X_LANES), I32)],
        scratch_shapes=[pltpu.VMEM((E, V7X_LANES), F32)],
        compiler_params=_cparams(("arbitrary",)),
        name="moe_rank",
    )(eidx)
    dest, be, nr = pl.pallas_call(
        functools.partial(_plan_kernel, E=E, bm=bm, NBP=NBP),
        grid=(N // tm,),
        in_specs=[pl.BlockSpec((K, tm), lambda i: (0, i)), pl.BlockSpec((K, tm), lambda i: (0, i)),
                  pl.BlockSpec((E, V7X_LANES), lambda i: (0, 0))],
        out_specs=[pl.BlockSpec((K, tm), lambda i: (0, i)), pl.BlockSpec((1, NBP), lambda i: (0, 0)),
                   pl.BlockSpec((1, NBP), lambda i: (0, 0))],
        out_shape=[jax.ShapeDtypeStruct((K, N), I32), jax.ShapeDtypeStruct((1, NBP), I32),
                   jax.ShapeDtypeStruct((1, NBP), I32)],
        compiler_params=_cparams(("arbitrary",)),
        name="moe_plan",
    )(eidx, rank, cnt)
    return dest, be.reshape(NBP), nr.reshape(NBP), NB


def _dispatch_kernel(dest_ref, hp_ref, hs_ref, xs_ref, sem, *, tm, tiles_p):
    i = pl.program_id(0)

    def scatter_rows(h_ref, rows):
        def row_copy(t, k):
            return pltpu.make_async_copy(h_ref.at[pl.ds(t, 1)], xs_ref.at[pl.ds(dest_ref[k, t], 1)], sem)

        def start(t, c):
            for k in range(TOP_K):
                row_copy(t, k).start()
            return c

        def wait(t, c):
            for k in range(TOP_K):
                row_copy(t, k).wait()
            return c

        lax.fori_loop(0, rows, start, 0, unroll=DMA_LOOP_UNROLL)
        lax.fori_loop(0, rows, wait, 0, unroll=DMA_LOOP_UNROLL)

    @pl.when(i < tiles_p)
    def _():
        scatter_rows(hp_ref, tm)

    @pl.when(i >= tiles_p)
    def _():
        scatter_rows(hs_ref, hs_ref.shape[0])


def moe_dispatch(h2_p, h2_s, dest, P, tm):
    D = h2_p.shape[1]
    NS = h2_s.shape[0]
    assert h2_p.shape[0] % tm == 0 and NS <= tm
    tiles_p = h2_p.shape[0] // tm
    assert dest.shape[1] == (tiles_p + 1) * tm
    return pl.pallas_call(
        functools.partial(_dispatch_kernel, tm=tm, tiles_p=tiles_p),
        grid=(tiles_p + 1,),
        in_specs=[pl.BlockSpec((TOP_K, tm), lambda i: (0, i), memory_space=pltpu.SMEM),
                  pl.BlockSpec((tm, D), lambda i: (jnp.minimum(i, tiles_p - 1), 0)),
                  pl.BlockSpec((NS, D), lambda i: (0, 0))],
        out_specs=pl.BlockSpec(memory_space=pl.ANY),
        out_shape=jax.ShapeDtypeStruct((P, D), F32),
        scratch_shapes=[pltpu.SemaphoreType.DMA(())],
        compiler_params=_cparams(("arbitrary",)),
        name="moe_dispatch",
    )(dest, h2_p, h2_s)


def _experts_kernel(be_ref, nr_ref, x_ref, wg_ref, wu_ref, wd_ref, y_ref, wgb_ref, wub_ref, wdb_ref):
    b = pl.program_id(0)
    prev = be_ref[jnp.maximum(b - 1, 0)]
    n = nr_ref[b]

    @pl.when((n > 0) & ((b == 0) | (be_ref[b] != prev)))
    def _():
        wgb_ref[...] = wg_ref[0].astype(BF16)
        wub_ref[...] = wu_ref[0].astype(BF16)
        wdb_ref[...] = wd_ref[0].astype(BF16)

    @pl.when(n > 0)
    def _():
        row = lax.broadcasted_iota(I32, (x_ref.shape[0], 1), 0)
        x = jnp.where(row < n, x_ref[...], 0.0).astype(BF16)
        gate = jnp.dot(x, wgb_ref[...], preferred_element_type=F32)
        up = jnp.dot(x, wub_ref[...], preferred_element_type=F32)
        y_ref[...] = jnp.dot((_silu(gate) * up).astype(BF16), wdb_ref[...], preferred_element_type=F32)

    @pl.when(n == 0)
    def _():
        y_ref[...] = jnp.zeros(y_ref.shape, F32)


def moe_experts(xs, be, nr, w_gate, w_up, w_down, NB, bm):
    P, D = xs.shape
    E, _, DE = w_gate.shape
    return pl.pallas_call(
        _experts_kernel,
        grid_spec=pltpu.PrefetchScalarGridSpec(
            num_scalar_prefetch=2,
            grid=(NB,),
            in_specs=[pl.BlockSpec((bm, D), lambda b, be, nr: (b, 0)),
                      pl.BlockSpec((1, D, DE), lambda b, be, nr: (be[b], 0, 0)),
                      pl.BlockSpec((1, D, DE), lambda b, be, nr: (be[b], 0, 0)),
                      pl.BlockSpec((1, DE, D), lambda b, be, nr: (be[b], 0, 0))],
            out_specs=pl.BlockSpec((bm, D), lambda b, be, nr: (b, 0)),
            scratch_shapes=[pltpu.VMEM((D, DE), BF16), pltpu.VMEM((D, DE), BF16), pltpu.VMEM((DE, D), BF16)]),
        out_shape=jax.ShapeDtypeStruct((P, D), F32),
        compiler_params=_cparams(("arbitrary",)),
        name="moe_experts",
    )(be, nr, xs, w_gate, w_up, w_down)


def _combine_kernel(dest_ref, ys_ref, wts_ref, x_ref, ysh_ref, gt_ref, gf_ref, o_ref, ybuf_ref, sem, *, tm):
    def row_copy(t, k):
        return pltpu.make_async_copy(ys_ref.at[pl.ds(dest_ref[k, t], 1)], ybuf_ref.at[k, pl.ds(t, 1)], sem)

    def start(t, c):
        for k in range(TOP_K):
            row_copy(t, k).start()
        return c

    def wait(t, c):
        for k in range(TOP_K):
            row_copy(t, k).wait()
        return c

    lax.fori_loop(0, tm, start, 0, unroll=DMA_LOOP_UNROLL)
    lax.fori_loop(0, tm, wait, 0, unroll=DMA_LOOP_UNROLL)
    w = wts_ref[...]
    ffn = ysh_ref[...]
    for k in range(TOP_K):
        ffn = ffn + w[:, k:k + 1] * ybuf_ref[k]
    x2 = x_ref[...] + gt_ref[...] * ffn
    ms = jnp.mean(x2 * x2, axis=-1, keepdims=True)
    o_ref[...] = x2 * lax.rsqrt(ms + NORM_EPS) * gf_ref[...]


def moe_combine(ys, dest, wts_t, x1, ysh, gt, g_final, tm, rows_per_group):
    N, D = x1.shape
    G, R, _ = gt.shape
    tpg = rows_per_group // tm
    return pl.pallas_call(
        functools.partial(_combine_kernel, tm=tm),
        grid=(N // tm,),
        in_specs=[pl.BlockSpec((TOP_K, tm), lambda i: (0, i), memory_space=pltpu.SMEM),
                  pl.BlockSpec(memory_space=pl.ANY),
                  pl.BlockSpec((tm, TOP_K), lambda i: (i, 0)),
                  pl.BlockSpec((tm, D), lambda i: (i, 0)),
                  pl.BlockSpec((tm, D), lambda i: (i, 0)),
                  pl.BlockSpec((None, R, D), lambda i: (i // tpg, 0, 0)),
                  pl.BlockSpec((1, D), lambda i: (0, 0))],
        out_specs=pl.BlockSpec((tm, D), lambda i: (i, 0)),
        out_shape=jax.ShapeDtypeStruct((N, D), F32),
        scratch_shapes=[pltpu.VMEM((TOP_K, tm, D), F32), pltpu.SemaphoreType.DMA(())],
        compiler_params=_cparams(("arbitrary",)),
        name="moe_combine",
    )(dest, ys, wts_t, x1, ysh, gt, g_final.reshape(1, D))


def make_cfg(x_prompt, x_sample, cache_k, cache_kidx, state_gla, page_table, w_in, w_gla_a, w_router, w_exp_gate):
    B, T, D = x_prompt.shape
    DB = x_sample.shape[0]
    assert x_sample.shape[1] == 1 and w_in.shape[0] == 1
    _, _, PAGE, NKV, HD = cache_k.shape
    ID = cache_kidx.shape[-1]
    _, _, GH, DK, DV = state_gla.shape
    RANK = w_gla_a.shape[1]
    NH = D // HD
    rest = w_in.shape[2] - (2 * GH * DK + 2 * GH * DV + RANK + NH * HD + 2 * NKV * HD + ID + 2 * D)
    IH = rest // (ID + 1)
    assert IH * (ID + 1) == rest
    NPAGES = page_table.shape[1]
    return Cfg(D=D, B=B, T=T, DB=DB, GH=GH, DK=DK, DV=DV, RANK=RANK, NH=NH, NKV=NKV, HD=HD, IH=IH, ID=ID,
               E=w_router.shape[2], DE=w_exp_gate.shape[3], PAGE=PAGE, NPAGES=NPAGES, PAST=NPAGES * PAGE)


def _row_tile(n, cap):
    t = cap
    while n % t:
        t //= 2
    return t


def kernel(x_prompt, x_sample, c_prompt, c_sample, cache_k, cache_v, cache_kidx, state_gla, page_table, w_ada, b_ada,
           g_norm1, w_in, w_gla_a, b_gla_a, g_gla_norm, w_gla_o, w_attn_o, w_out, g_norm2, w_router, b_router,
           w_sh_gate, w_sh_up, w_sh_down, w_exp_gate, w_exp_up, w_exp_down, g_final):
    cfg = make_cfg(x_prompt, x_sample, cache_k, cache_kidx, state_gla, page_table, w_in, w_gla_a, w_router, w_exp_gate)
    lay = proj_layout(cfg)
    B, T, D, DB = cfg.B, cfg.T, cfg.D, cfg.DB
    NP_, NS = B * T, DB
    kvw = cfg.NKV * cfg.HD

    pad = (-(B + DB)) % V7X_SUBLANES
    c_all = jnp.concatenate([c_prompt, c_sample, jnp.zeros((pad, D), F32)], axis=0)
    mod = ada_mod(c_all, w_ada[0], b_ada[0])
    sh1p, sc1p, gt1p, sh2p, sc2p, gt2p = [m[:, None, :] for m in jnp.split(mod[:B], 6, axis=-1)]
    sh1s, sc1s, gt1s, sh2s, sc2s, gt2s = [m[None] for m in jnp.split(mod[B:B + DB], 6, axis=-1)]

    w_pad = relayout_w_in(w_in[0], cfg, lay)
    tm_p = _row_tile(T, 1024)
    xp = x_prompt.reshape(NP_, D)
    xs = x_sample.reshape(NS, D)
    proj_p = in_proj(xp, g_norm1[0], sc1p, sh1p, w_pad, rope_tables(np.arange(T), cfg), cfg, lay, tm_p, T)
    proj_s = in_proj(xs, g_norm1[0], sc1s, sh1s, w_pad, rope_tables(np.full((NS,), cfg.PAST), cfg), cfg, lay, NS, NS)

    def grp(proj, n, w=None):
        return proj[:, lay.offs[n]:lay.offs[n] + (lay.widths[n] if w is None else w)]

    proj_p3 = proj_p.reshape(B, T, lay.total)
    og_p, gla_p = gla_prompt(proj_p3, w_gla_a[0], b_gla_a[0], g_gla_norm[0], cfg, lay)
    og_s, gla_s = gla_sample(proj_s.reshape(DB, 1, lay.total), state_gla[0], w_gla_a[0], b_gla_a[0], g_gla_norm[0],
                             cfg, lay)

    k_p, v_p, kidx_p = grp(proj_p, "ak"), grp(proj_p, "av"), grp(proj_p, "ik", cfg.ID)
    k_s, v_s, kidx_s = grp(proj_s, "ak"), grp(proj_s, "av"), grp(proj_s, "ik", cfg.ID)
    oa_p = dsa_prompt(proj_p3, k_p.astype(BF16).reshape(B, T, kvw), v_p.astype(BF16).reshape(B, T, kvw), cfg, lay)
    oa_s = dsa_sample(grp(proj_s, "aq").reshape(DB, cfg.NH, cfg.HD), grp(proj_s, "iq").reshape(DB, cfg.IH, cfg.ID),
                      grp(proj_s, "iw", cfg.IH), kidx_s, k_s, v_s, cache_k, cache_v, cache_kidx, page_table, cfg)

    wgo, wao, wo = w_gla_o[0].astype(BF16), w_attn_o[0].astype(BF16), w_out[0].astype(BF16)
    mg_p = merge_branches(og_p.reshape(NP_, -1), oa_p.reshape(NP_, -1), wgo, wao, proj_p, cfg, lay, _row_tile(T, 512))
    mg_s = merge_branches(og_s.reshape(NS, -1), oa_s.reshape(NS, -1), wgo, wao, proj_s, cfg, lay, NS)
    x1_p = out_proj(mg_p, wo, xp, gt1p, _row_tile(T, 1024), T, lay.tn)
    x1_s = out_proj(mg_s, wo, xs, gt1s, NS, NS, lay.tn)

    tile = V7X_LANES
    assert NP_ % tile == 0 and NS <= tile
    n_tok = NP_ + tile
    wr_t = w_router[0].T
    tm_f = _row_tile(T, 256)
    h2_p, ysh_p, eidx_p, wts_p = ffn_pre(x1_p, g_norm2[0], sc2p, sh2p, wr_t, b_router[0], w_sh_gate[0], w_sh_up[0],
                                         w_sh_down[0], cfg, tm_f)
    h2_s, ysh_s, eidx_s, wts_s = ffn_pre(x1_s, g_norm2[0], sc2s, sh2s, wr_t, b_router[0], w_sh_gate[0], w_sh_up[0],
                                         w_sh_down[0], cfg, NS)
    eidx = jnp.concatenate([eidx_p, eidx_s, jnp.full((TOP_K, n_tok - NP_ - NS), -1, I32)], axis=1)
    bm = 256 if NP_ * TOP_K >= 256 * cfg.E * 4 else 128
    dest, be, nr, NB = moe_plan(eidx, cfg, bm, tile)
    xs_sorted = moe_dispatch(h2_p, h2_s, dest, NB * bm, tile)
    ys_sorted = moe_experts(xs_sorted, be, nr, w_exp_gate[0], w_exp_up[0], w_exp_down[0], NB, bm)
    tm_c = _row_tile(T, 128)
    y_p = moe_combine(ys_sorted, dest[:, :NP_], wts_p.T, x1_p, ysh_p, gt2p, g_final, tm_c, T)
    y_s = moe_combine(ys_sorted, dest[:, NP_:NP_ + NS], wts_s.T, x1_s, ysh_s, gt2s, g_final, NS, NS)

    return (y_p.reshape(B, T, D), y_s.reshape(DB, 1, D),
            k_p.reshape(1, B, T, cfg.NKV, cfg.HD), v_p.reshape(1, B, T, cfg.NKV, cfg.HD), kidx_p.reshape(1, B, T, cfg.ID),
            gla_p[None],
            k_s.reshape(1, DB, 1, cfg.NKV, cfg.HD), v_s.reshape(1, DB, 1, cfg.NKV, cfg.HD), kidx_s.reshape(1, DB, 1, cfg.ID),
            gla_s[None])
```

```python
import functools
import math
from typing import NamedTuple

import jax
import jax.numpy as jnp
import numpy as np
from jax import lax
from jax.experimental import pallas as pl
from jax.experimental.pallas import tpu as pltpu

F32 = jnp.float32
BF16 = jnp.bfloat16
I32 = jnp.int32

V7X_LANES = 128
V7X_SUBLANES = 8
V7X_VMEM_LIMIT_BYTES = 56 * 1024 * 1024

GLA_TAU = 16.0
GLA_CHUNK = 64
ROPE_THETA = 500000.0
TOPK_MAX = 256
Q_BLOCK = 128
KEY_CHUNK = 1024
DMA_LOOP_UNROLL = 4
N_GROUPS = 8
TOPK_GROUPS = 4
TOP_K = 8
ROUTED_SCALE = 2.5
NORM_EPS = 1e-6
INT_MIN = -(2 ** 31)
INT_MAX = 2 ** 31 - 1
NEG_BIG = -1e30


class Cfg(NamedTuple):
    D: int
    B: int
    T: int
    DB: int
    GH: int
    DK: int
    DV: int
    RANK: int
    NH: int
    NKV: int
    HD: int
    IH: int
    ID: int
    E: int
    DE: int
    PAGE: int
    NPAGES: int
    PAST: int


def _cparams(sem, vmem=V7X_VMEM_LIMIT_BYTES):
    return pltpu.CompilerParams(dimension_semantics=sem, vmem_limit_bytes=vmem)


def _bdot(a, b):
    return jnp.dot(a.astype(BF16), b.astype(BF16), preferred_element_type=F32)


def _dot_nt(a, b):
    return lax.dot_general(a.astype(BF16), b.astype(BF16), (((1,), (1,)), ((), ())),
                           preferred_element_type=F32)


def _dot_tn(a, b):
    return lax.dot_general(a.astype(BF16), b.astype(BF16), (((0,), (0,)), ((), ())),
                           preferred_element_type=F32)


def _split2(x):
    hi = x.astype(BF16)
    lo = (x - hi.astype(F32)).astype(BF16)
    return hi, lo


def _split3(x):
    hi = x.astype(BF16)
    r1 = x - hi.astype(F32)
    mid = r1.astype(BF16)
    lo = (r1 - mid.astype(F32)).astype(BF16)
    return hi, mid, lo


def _sigmoid(x):
    return 1.0 / (1.0 + jnp.exp(-x))


def _silu(x):
    return x * _sigmoid(x)


def _ada_kernel(c_ref, w_ref, b_ref, o_ref):
    c = c_ref[...]
    o_ref[...] = _bdot(_silu(c), w_ref[...]) + b_ref[...]


def ada_mod(c, w_ada, b_ada):
    R, D = c.shape
    N = w_ada.shape[1]
    tn = 512 if N % 512 == 0 else 128
    return pl.pallas_call(
        _ada_kernel,
        grid=(N // tn,),
        in_specs=[pl.BlockSpec((R, D), lambda j: (0, 0)),
                  pl.BlockSpec((D, tn), lambda j: (0, j)),
                  pl.BlockSpec((1, tn), lambda j: (0, j))],
        out_specs=pl.BlockSpec((R, tn), lambda j: (0, j)),
        out_shape=jax.ShapeDtypeStruct((R, N), F32),
        compiler_params=_cparams(("arbitrary",)),
        name="ada_mod",
    )(c, w_ada, b_ada.reshape(1, N))


class ProjLayout(NamedTuple):
    names: tuple
    offs: dict
    widths: dict
    total: int
    tn: int
    chunk_types: tuple


def proj_layout(cfg: Cfg) -> ProjLayout:
    qk = cfg.GH * cfg.DK
    vv = cfg.GH * cfg.DV
    aq = cfg.NH * cfg.HD
    kv = cfg.NKV * cfg.HD
    iq = cfg.IH * cfg.ID
    big = [("gq", qk), ("gk", qk), ("gv", vv), ("gr", vv), ("aq", aq), ("ak", kv), ("av", kv),
           ("iq", iq), ("za", cfg.D), ("zb", cfg.D)]
    tn = 512
    for _, w in big:
        tn = math.gcd(tn, w)
    assert tn % V7X_LANES == 0
    offs, widths, types = {}, {}, []
    off = 0
    for n, w in big:
        offs[n], widths[n] = off, w
        t = {"aq": 1, "ak": 1, "iq": 2, "za": 3, "zb": 3}.get(n, 0)
        types += [t] * (w // V7X_LANES)
        off += w
    for n, t in (("ga", 0), ("ik", 2), ("iw", 0), ("pad", 0)):
        offs[n], widths[n] = off, V7X_LANES
        types.append(t)
        off += V7X_LANES
    assert off % tn == 0
    return ProjLayout(tuple(n for n, _ in big) + ("ga", "ik", "iw", "pad"), offs, widths, off, tn, tuple(types))


def w_in_views(w_in, cfg: Cfg, lay: ProjLayout):
    qk = cfg.GH * cfg.DK
    vv = cfg.GH * cfg.DV
    aq = cfg.NH * cfg.HD
    kv = cfg.NKV * cfg.HD
    iq = cfg.IH * cfg.ID
    src_w = dict(gq=qk, gk=qk, gv=vv, ga=cfg.RANK, gr=vv, aq=aq, ak=kv, av=kv, iq=iq, ik=cfg.ID, iw=cfg.IH,
                 za=cfg.D, zb=cfg.D)
    src_off, o = {}, 0
    for n in ("gq", "gk", "gv", "ga", "gr", "aq", "ak", "av", "iq", "ik", "iw", "za", "zb"):
        src_off[n] = o
        o += src_w[n]
    assert o == w_in.shape[1]
    D = w_in.shape[0]
    wt = jnp.swapaxes(w_in, 0, 1)

    def rows(n):
        return wt[src_off[n]:src_off[n] + src_w[n]]

    def padded(a):
        return jnp.pad(a, ((0, V7X_LANES - a.shape[0]), (0, 0)))

    assert 2 * cfg.ID == V7X_LANES and cfg.RANK <= V7X_LANES and cfg.IH <= V7X_LANES
    small = jnp.concatenate([padded(rows("ga")), rows("ik"), rows("ik"), padded(rows("iw")),
                             jnp.zeros((V7X_LANES, D), w_in.dtype)], axis=0).astype(BF16)
    tile_src = []
    for n in ("gq", "gk", "gv", "gr", "aq", "ak", "av", "iq", "za", "zb"):
        assert src_off[n] % V7X_SUBLANES == 0 and lay.widths[n] == src_w[n]
        tile_src += [src_off[n] + t * lay.tn for t in range(src_w[n] // lay.tn)]
    return wt, small, np.asarray(tile_src, np.int32)


def rope_tables(pos, cfg: Cfg):
    pos = np.asarray(pos, np.float64)

    def one(width, rot):
        half = rot // 2
        inv = ROPE_THETA ** (-np.arange(half, dtype=np.float64) * 2.0 / rot)
        ang = pos[:, None] * inv[None, :]
        lane = np.arange(V7X_LANES) % width
        c = np.where(lane < rot, np.cos(ang)[:, lane % half], 1.0)
        s1 = np.where((lane >= half) & (lane < rot), np.sin(ang)[:, (lane - half) % half], 0.0)
        s2 = np.where(lane < half, -np.sin(ang)[:, lane % half], 0.0)
        return [c, s1, s2]

    tabs = one(cfg.HD, cfg.HD // 4) + one(cfg.ID, cfg.ID // 4)
    return jnp.asarray(np.concatenate(tabs, axis=1), F32)


def _inproj_kernel(src_ref, x_ref, g_ref, sc_ref, sh_ref, ws_ref, rope_ref, wt_hbm, o_ref, h_ref, wbuf, acc_ref, sem,
                   *, patterns, half_a, half_i, n_main):
    i = pl.program_id(0)
    j = pl.program_id(1)
    ni = pl.num_programs(0)
    tn = wbuf.shape[1]
    seq = i * n_main + j

    def tile_copy(jj, slot):
        return pltpu.make_async_copy(wt_hbm.at[pl.ds(pl.multiple_of(src_ref[jj], V7X_SUBLANES), tn)],
                                     wbuf.at[slot], sem.at[slot])

    @pl.when((i == 0) & (j == 0))
    def _():
        tile_copy(0, 0).start()

    @pl.when(j == 0)
    def _():
        x = x_ref[...]
        ms = jnp.mean(x * x, axis=-1, keepdims=True)
        y = x * lax.rsqrt(ms + NORM_EPS) * g_ref[...]
        h_ref[...] = (y * (1.0 + sc_ref[...]) + sh_ref[...]).astype(BF16)

    dn = (((1,), (1,)), ((), ()))

    @pl.when(j < n_main)
    def _():
        slot = seq % 2
        tile_copy(j, slot).wait()

        @pl.when(j + 1 < n_main)
        def _():
            tile_copy(j + 1, 1 - slot).start()

        @pl.when((j + 1 == n_main) & (i + 1 < ni))
        def _():
            tile_copy(0, 1 - slot).start()

        acc_ref[...] = lax.dot_general(h_ref[...], wbuf[slot].astype(BF16), dn, preferred_element_type=F32)

    @pl.when(j >= n_main)
    def _():
        acc_ref[...] = lax.dot_general(h_ref[...], ws_ref[...], dn, preferred_element_type=F32)

    acc = acc_ref[...]
    L = V7X_LANES

    def rot(a, t):
        base, half = (0, half_a) if t == 1 else (3 * L, half_i)
        c = rope_ref[:, base:base + L]
        s1 = rope_ref[:, base + L:base + 2 * L]
        s2 = rope_ref[:, base + 2 * L:base + 3 * L]
        return a * c + pltpu.roll(a, half, 1) * s1 + pltpu.roll(a, L - half, 1) * s2

    for pat, ranges in patterns:
        cond = None
        for lo, hi in ranges:
            c = (j >= lo) & (j < hi)
            cond = c if cond is None else (cond | c)

        @pl.when(cond)
        def _(pat=pat):
            if all(t == 0 for t in pat):
                o_ref[...] = acc
            elif all(t == 3 for t in pat):
                o_ref[...] = _sigmoid(acc)
            else:
                for ci, t in enumerate(pat):
                    a = acc[:, ci * L:(ci + 1) * L]
                    o_ref[:, ci * L:(ci + 1) * L] = a if t == 0 else rot(a, t)


def _tile_patterns(lay: ProjLayout):
    per = lay.tn // V7X_LANES
    ntiles = lay.total // lay.tn
    pats = {}
    for j in range(ntiles):
        pats.setdefault(tuple(lay.chunk_types[j * per:(j + 1) * per]), []).append(j)
    out = []
    for pat, js in pats.items():
        ranges, start, prev = [], js[0], js[0]
        for j in js[1:]:
            if j != prev + 1:
                ranges.append((start, prev + 1))
                start = j
            prev = j
        ranges.append((start, prev + 1))
        out.append((pat, tuple(ranges)))
    return tuple(out)


def in_proj(x, g, sc, sh, w_views, rope, cfg: Cfg, lay: ProjLayout, tm, rows_per_group):
    wt, w_small, tile_src = w_views
    N, D = x.shape
    G, R, _ = sc.shape
    tn = lay.tn
    tiles_per_group = rows_per_group // tm
    n_main = len(tile_src)
    assert N == G * rows_per_group and rows_per_group % tm == 0 and R in (1, tm)
    assert rope.shape[0] == rows_per_group and n_main * tn + w_small.shape[0] == lay.total
    kern = functools.partial(_inproj_kernel, patterns=_tile_patterns(lay), half_a=cfg.HD // 8, half_i=cfg.ID // 8,
                             n_main=n_main)
    mod_spec = pl.BlockSpec((None, R, D), lambda i, j, s: (i // tiles_per_group, 0, 0))
    return pl.pallas_call(
        kern,
        grid_spec=pltpu.PrefetchScalarGridSpec(
            num_scalar_prefetch=1,
            grid=(N // tm, lay.total // tn),
            in_specs=[pl.BlockSpec((tm, D), lambda i, j, s: (i, 0)),
                      pl.BlockSpec((1, D), lambda i, j, s: (0, 0)),
                      mod_spec, mod_spec,
                      pl.BlockSpec((tn, D), lambda i, j, s: (jnp.maximum(j - n_main, 0), 0)),
                      pl.BlockSpec((tm, 6 * V7X_LANES), lambda i, j, s: (i % tiles_per_group, 0)),
                      pl.BlockSpec(memory_space=pl.ANY)],
            out_specs=pl.BlockSpec((tm, tn), lambda i, j, s: (i, j)),
            scratch_shapes=[pltpu.VMEM((tm, D), BF16), pltpu.VMEM((2, tn, D), F32), pltpu.VMEM((tm, tn), F32),
                            pltpu.SemaphoreType.DMA((2,))]),
        out_shape=jax.ShapeDtypeStruct((N, lay.total), F32),
        compiler_params=_cparams(("arbitrary", "arbitrary")),
        name="in_proj",
    )(jnp.asarray(tile_src), x, g.reshape(1, D), sc, sh, w_small, rope, wt)


def _log_sigmoid(z):
    return jnp.minimum(z, 0.0) - jnp.log1p(jnp.exp(-jnp.abs(z)))


def _gla_prompt_kernel(q_ref, k_ref, v_ref, r_ref, a_ref, wa_ref, ba_ref, gn_ref, o_ref, s_ref, st_ref,
                       *, C, nchunk, rank, scale):
    t = pl.program_id(2)

    @pl.when(t == 0)
    def _():
        st_ref[...] = jnp.zeros_like(st_ref)

    row = lax.broadcasted_iota(I32, (C, C), 0)
    col = lax.broadcasted_iota(I32, (C, C), 1)
    causal = row >= col
    tri = jnp.where(causal, 1.0, 0.0).astype(BF16)
    wa = wa_ref[...]
    ba = ba_ref[...]
    gn = gn_ref[...]
    for c in range(nchunk):
        sl = slice(c * C, (c + 1) * C)
        q = q_ref[0, sl, :]
        k = k_ref[0, sl, :]
        v = v_ref[0, sl, :]
        a = a_ref[0, sl, :][:, :rank]
        la = _log_sigmoid(_bdot(a, wa) + ba) * (1.0 / GLA_TAU)
        hi, mid, lo = _split3(la)
        bc = (jnp.dot(tri, hi, preferred_element_type=F32) + jnp.dot(tri, mid, preferred_element_type=F32)
              + jnp.dot(tri, lo, preferred_element_type=F32))
        bend = bc[C - 1:C, :]
        qd = q * scale * jnp.exp(bc)
        kd = k * jnp.exp(-bc)
        ke = k * jnp.exp(bend - bc)
        att = jnp.where(causal, _dot_nt(qd, kd), 0.0)
        st = st_ref[...]
        o = _bdot(att, v) + _dot_nt(qd, st)
        st_ref[...] = st * jnp.exp(bend) + _dot_tn(v, ke)
        ms = jnp.mean(o * o, axis=-1, keepdims=True)
        on = o * lax.rsqrt(ms + NORM_EPS) * gn
        o_ref[0, sl, :] = on * _silu(r_ref[0, sl, :])

    @pl.when(t == pl.num_programs(2) - 1)
    def _():
        s_ref[0, 0] = st_ref[...].T


def gla_prompt(proj, w_gla_a, b_gla_a, g_gla_norm, cfg: Cfg, lay: ProjLayout):
    B, T, _ = proj.shape
    C = min(GLA_CHUNK, T)
    assert T % C == 0
    tc = C
    while tc * 2 <= 512 and T % (tc * 2) == 0:
        tc *= 2
    DK, DV, GH = cfg.DK, cfg.DV, cfg.GH
    oq, ok, ov, orr, oa = (lay.offs[n] for n in ("gq", "gk", "gv", "gr", "ga"))
    assert oq % DK == 0 and ok % DK == 0 and ov % DV == 0 and orr % DV == 0 and oa % V7X_LANES == 0
    kern = functools.partial(_gla_prompt_kernel, C=C, nchunk=tc // C, rank=cfg.RANK, scale=DK ** -0.5)
    return pl.pallas_call(
        kern,
        grid=(B, GH, T // tc),
        in_specs=[pl.BlockSpec((1, tc, DK), lambda b, h, t: (b, t, oq // DK + h)),
                  pl.BlockSpec((1, tc, DK), lambda b, h, t: (b, t, ok // DK + h)),
                  pl.BlockSpec((1, tc, DV), lambda b, h, t: (b, t, ov // DV + h)),
                  pl.BlockSpec((1, tc, DV), lambda b, h, t: (b, t, orr // DV + h)),
                  pl.BlockSpec((1, tc, V7X_LANES), lambda b, h, t: (b, t, oa // V7X_LANES)),
                  pl.BlockSpec((cfg.RANK, DK), lambda b, h, t: (0, h)),
                  pl.BlockSpec((1, DK), lambda b, h, t: (0, h)),
                  pl.BlockSpec((1, DV), lambda b, h, t: (0, 0))],
        out_specs=[pl.BlockSpec((1, tc, DV), lambda b, h, t: (b, t, h)),
                   pl.BlockSpec((1, 1, DK, DV), lambda b, h, t: (b, h, 0, 0))],
        out_shape=[jax.ShapeDtypeStruct((B, T, GH * DV), F32),
                   jax.ShapeDtypeStruct((B, GH, DK, DV), F32)],
        scratch_shapes=[pltpu.VMEM((DV, DK), F32)],
        compiler_params=_cparams(("arbitrary", "arbitrary", "arbitrary")),
        name="gla_prompt",
    )(proj, proj, proj, proj, proj, w_gla_a, b_gla_a.reshape(1, -1), g_gla_norm.reshape(1, DV))


def _gla_sample_kernel(q_ref, k_ref, v_ref, r_ref, a_ref, wa_ref, ba_ref, gn_ref, s0_ref, o_ref, s_ref,
                       *, GH, DK, DV, rank, scale):
    a = jnp.broadcast_to(a_ref[0][:, :rank], (V7X_SUBLANES, rank))
    ones = jnp.ones((V7X_SUBLANES, V7X_LANES), BF16)
    first = lax.broadcasted_iota(I32, (V7X_SUBLANES, 1), 0) == 0
    for h in range(GH):
        q = q_ref[0][:, h * DK:(h + 1) * DK]
        k = k_ref[0][:, h * DK:(h + 1) * DK]
        v = v_ref[0][:, h * DV:(h + 1) * DV]
        la = _log_sigmoid(_bdot(a, wa_ref[:, h * DK:(h + 1) * DK]) + ba_ref[:, h * DK:(h + 1) * DK])
        la = la[0:1, :] * (1.0 / GLA_TAU)
        dec = jnp.exp(la)
        qd = q * scale * dec
        kd = k * jnp.exp(-la)
        att = jnp.sum(qd * kd, axis=-1, keepdims=True)
        s0 = s0_ref[0, h]
        qd8 = jnp.broadcast_to(qd, (V7X_SUBLANES, DK))
        qh, ql = _split2(qd8)
        sh, sl = _split2(s0)
        qs = (jnp.dot(qh, sh, preferred_element_type=F32) + jnp.dot(qh, sl, preferred_element_type=F32)
              + jnp.dot(ql, sh, preferred_element_type=F32))[0:1, :]
        o = att * v + qs
        def col(x, width):
            x8 = jnp.where(first, jnp.broadcast_to(x, (V7X_SUBLANES, x.shape[1])), 0.0)
            parts = _split3(x8)
            rhs = ones if width == V7X_LANES else jnp.ones((V7X_SUBLANES, width), BF16)
            acc = None
            for p in parts:
                d = lax.dot_general(p, rhs, (((0,), (0,)), ((), ())), preferred_element_type=F32)
                acc = d if acc is None else acc + d
            return acc

        dec_col = col(dec, V7X_LANES)[:, 0:1]
        k8 = jnp.where(first, jnp.broadcast_to(k, (V7X_SUBLANES, DK)), 0.0)
        v8 = jnp.where(first, jnp.broadcast_to(v, (V7X_SUBLANES, DV)), 0.0)
        kh, kl = _split2(k8)
        vh, vl = _split2(v8)
        dn = (((0,), (0,)), ((), ()))
        outer = (lax.dot_general(kh, vh, dn, preferred_element_type=F32)
                 + lax.dot_general(kh, vl, dn, preferred_element_type=F32)
                 + lax.dot_general(kl, vh, dn, preferred_element_type=F32))
        s_ref[0, h] = s0 * dec_col + outer
        ms = jnp.mean(o * o, axis=-1, keepdims=True)
        on = o * lax.rsqrt(ms + NORM_EPS) * gn_ref[...]
        o_ref[0, :, h * DV:(h + 1) * DV] = on * _silu(r_ref[0][:, h * DV:(h + 1) * DV])


def gla_sample(proj, state, w_gla_a, b_gla_a, g_gla_norm, cfg: Cfg, lay: ProjLayout):
    DB = proj.shape[0]
    DK, DV, GH = cfg.DK, cfg.DV, cfg.GH
    qk, vv = GH * DK, GH * DV
    oq, ok, ov, orr, oa = (lay.offs[n] for n in ("gq", "gk", "gv", "gr", "ga"))
    assert oq % qk == 0 and ok % qk == 0 and ov % vv == 0 and orr % vv == 0
    kern = functools.partial(_gla_sample_kernel, GH=GH, DK=DK, DV=DV, rank=cfg.RANK, scale=DK ** -0.5)
    return pl.pallas_call(
        kern,
        grid=(DB,),
        in_specs=[pl.BlockSpec((1, 1, qk), lambda b: (b, 0, oq // qk)),
                  pl.BlockSpec((1, 1, qk), lambda b: (b, 0, ok // qk)),
                  pl.BlockSpec((1, 1, vv), lambda b: (b, 0, ov // vv)),
                  pl.BlockSpec((1, 1, vv), lambda b: (b, 0, orr // vv)),
                  pl.BlockSpec((1, 1, V7X_LANES), lambda b: (b, 0, oa // V7X_LANES)),
                  pl.BlockSpec((cfg.RANK, qk), lambda b: (0, 0)),
                  pl.BlockSpec((1, qk), lambda b: (0, 0)),
                  pl.BlockSpec((1, DV), lambda b: (0, 0)),
                  pl.BlockSpec((1, GH, DK, DV), lambda b: (b, 0, 0, 0))],
        out_specs=[pl.BlockSpec((1, 1, vv), lambda b: (b, 0, 0)),
                   pl.BlockSpec((1, GH, DK, DV), lambda b: (b, 0, 0, 0))],
        out_shape=[jax.ShapeDtypeStruct((DB, 1, vv), F32),
                   jax.ShapeDtypeStruct((DB, GH, DK, DV), F32)],
        compiler_params=_cparams(("arbitrary",)),
        name="gla_sample",
    )(proj, proj, proj, proj, proj, w_gla_a, b_gla_a.reshape(1, -1), g_gla_norm.reshape(1, DV), state)


def _offset_word_to_f32(t):
    key = t ^ INT_MIN
    return pltpu.bitcast(key ^ ((key >> 31) & INT_MAX), F32)


def _dsa_prompt_kernel(iq_ref, iw_ref, aq_ref, kidx_ref, kbf_ref, vbf_ref, o_ref,
                       sc_ref, lg_ref, q2_ref, wb_ref, *, IH, ID, NH, G, HD, topk, wscale, sm_scale, KEY_CHUNK):
    qb = pl.program_id(1)
    QB = Q_BLOCK
    L = V7X_LANES
    CH = KEY_CHUNK // L
    nch = qb // CH + 1
    lane = lax.broadcasted_iota(I32, (QB, L), 1)
    rowq = lax.broadcasted_iota(I32, (QB, 1), 0) + qb * QB
    first = lane < ID

    for h in range(IH):
        pair = iq_ref[0][:, (h // 2) * L:(h // 2 + 1) * L]
        qq = jnp.where((lane // ID) == (h % 2), pair, pltpu.roll(pair, ID, 1))
        hi, lo = _split2(qq)
        q2_ref[h * QB:(h + 1) * QB, 0:L] = jnp.where(first, hi, lo)
        q2_ref[h * QB:(h + 1) * QB, L:2 * L] = jnp.where(first, hi, jnp.zeros_like(hi))
        wb_ref[h * QB:(h + 1) * QB, :] = jnp.broadcast_to(iw_ref[0][:, h:h + 1] * wscale, (QB, 2 * L))

    lane2 = lax.broadcasted_iota(I32, (QB, 2 * L), 1)
    first_k = lax.broadcasted_iota(I32, (2 * L, L), 1) < ID

    def score_body(c, carry):
        for half in range(CH // 2):
            k0 = c * KEY_CHUNK + half * 2 * L
            cols = slice(half * 2 * L, (half + 1) * 2 * L)

            @pl.when(k0 < (qb + 1) * QB)
            def _():
                kblk = kidx_ref[0, pl.ds(pl.multiple_of(k0, 2 * L), 2 * L), :]
                khi, klo = _split2(kblk)
                k2 = jnp.concatenate([khi, jnp.where(first_k, klo, jnp.zeros_like(klo))], axis=1)
                s = lax.dot_general(q2_ref[...], k2, (((1,), (1,)), ((), ())),
                                    preferred_element_type=F32)
                s = jnp.maximum(s, 0.0) * wb_ref[...]
                sc = s[0:QB]
                for h in range(1, IH):
                    sc = sc + s[h * QB:(h + 1) * QB]
                sc_ref[c, :, cols] = jnp.where(k0 + lane2 <= rowq, sc, -jnp.inf)

            @pl.when(k0 >= (qb + 1) * QB)
            def _():
                sc_ref[c, :, cols] = jnp.full((QB, 2 * L), -jnp.inf, F32)
        return carry

    lax.fori_loop(0, nch, score_body, 0)

    def sweep(fn, init, combine):
        def body(c, acc):
            slab = sc_ref[c]
            for j in range(CH):
                acc = combine(acc, fn(slab[:, j * L:(j + 1) * L], c * KEY_CHUNK + j * L + lane))
            return acc
        return lax.fori_loop(0, nch, body, init)

    def count(pred):
        acc = sweep(lambda s, pos: jnp.where(pred(s, pos), 1, 0), jnp.zeros((QB, L), I32), jnp.add)
        return jnp.sum(acc, axis=1, keepdims=True)

    want = jnp.minimum(rowq + 1, topk)

    def bit_body(i, t):
        cand = t | (jnp.int32(1) << (31 - i))
        cf = _offset_word_to_f32(cand)
        cnt = count(lambda s, pos: s >= cf)
        return jnp.where(cnt >= want, cand, t)

    thr0 = _offset_word_to_f32(lax.fori_loop(0, 32, bit_body, jnp.zeros((QB, 1), I32)))
    n_ge = count(lambda s, pos: s >= thr0)
    thr = jnp.min(sweep(lambda s, pos: jnp.where(s >= thr0, s, jnp.inf), jnp.full((QB, L), jnp.inf, F32),
                        jnp.minimum), axis=1, keepdims=True)
    cut = jnp.full((QB, 1), INT_MAX, I32)

    def selected(s, pos, thr, cut):
        return (s > thr) | ((s == thr) & (pos <= cut))

    def drop_last(_, tc):
        thr, cut = tc
        over = count(lambda s, pos: selected(s, pos, thr, cut)) > want
        low = jnp.min(sweep(lambda s, pos: jnp.where(selected(s, pos, thr, cut), s, jnp.inf),
                            jnp.full((QB, L), jnp.inf, F32), jnp.minimum), axis=1, keepdims=True)
        last = jnp.max(sweep(lambda s, pos: jnp.where(selected(s, pos, thr, cut) & (s == low), pos, -1),
                             jnp.full((QB, L), -1, I32), jnp.maximum), axis=1, keepdims=True)
        return jnp.where(over, low, thr), jnp.where(over, last - 1, cut)

    thr, cut = lax.fori_loop(0, jnp.max(n_ge - want), drop_last, (thr, cut))

    def mask_body(c, carry):
        slab = sc_ref[c]
        pos = c * KEY_CHUNK + lax.broadcasted_iota(I32, slab.shape, 1)
        sc_ref[c] = jnp.where(selected(slab, pos, thr, cut), 0.0, -jnp.inf)
        return carry

    lax.fori_loop(0, nch, mask_body, 0)

    c2 = sm_scale * math.log2(math.e)
    for g in range(NH // G):
        qg = jnp.concatenate([aq_ref[0][:, (g * G + j) * HD:(g * G + j + 1) * HD] for j in range(G)],
                             axis=0).astype(BF16)

        def logit_body(c, mx):
            rows = pl.ds(pl.multiple_of(c * KEY_CHUNK, KEY_CHUNK), KEY_CHUNK)
            s = lax.dot_general(qg, kbf_ref[0, rows, g * HD:(g + 1) * HD], (((1,), (1,)), ((), ())),
                                preferred_element_type=F32) * c2
            mask = sc_ref[c]
            for j in range(G):
                sj = s[j * QB:(j + 1) * QB] + mask
                lg_ref[c, j * QB:(j + 1) * QB, :] = sj
                for b in range(CH):
                    mx[j] = jnp.maximum(mx[j], sj[:, b * L:(b + 1) * L])
            return mx

        mx = lax.fori_loop(0, nch, logit_body, [jnp.full((QB, L), NEG_BIG, F32) for _ in range(G)])
        m = jnp.concatenate([jnp.broadcast_to(jnp.max(x, axis=1, keepdims=True), (QB, L)) for x in mx],
                            axis=0)

        def value_body(c, carry):
            ls, acc = carry
            rows = pl.ds(pl.multiple_of(c * KEY_CHUNK, KEY_CHUNK), KEY_CHUNK)
            ps = [jnp.exp2(lg_ref[c, :, b * L:(b + 1) * L] - m) for b in range(CH)]
            for p in ps:
                ls = ls + p
            p = jnp.concatenate(ps, axis=1).astype(BF16)
            acc = acc + jnp.dot(p, vbf_ref[0, rows, g * HD:(g + 1) * HD], preferred_element_type=F32)
            return ls, acc

        ls, acc = lax.fori_loop(0, nch, value_body, (jnp.zeros((G * QB, L), F32), jnp.zeros((G * QB, HD), F32)))
        out = acc / jnp.sum(ls, axis=1, keepdims=True)
        for j in range(G):
            o_ref[0, :, (g * G + j) * HD:(g * G + j + 1) * HD] = out[j * QB:(j + 1) * QB]


def dsa_prompt(proj, kbf, vbf, cfg: Cfg, lay: ProjLayout):
    B, T, _ = proj.shape
    QB = Q_BLOCK
    chunk = min(KEY_CHUNK, T)
    assert T % chunk == 0 and chunk % (2 * V7X_LANES) == 0 and cfg.ID * 2 == V7X_LANES and cfg.IH % 2 == 0
    topk = min(TOPK_MAX, T // 4)
    iqw, aqw, kvw = cfg.IH * cfg.ID, cfg.NH * cfg.HD, cfg.NKV * cfg.HD
    oiq, oaq, oik, oiw = (lay.offs[n] for n in ("iq", "aq", "ik", "iw"))
    assert oiq % iqw == 0 and oaq % aqw == 0
    G = cfg.NH // cfg.NKV
    kern = functools.partial(_dsa_prompt_kernel, IH=cfg.IH, ID=cfg.ID, NH=cfg.NH, G=G, HD=cfg.HD, topk=topk,
                             wscale=(cfg.IH ** -0.5) * (cfg.ID ** -0.5), sm_scale=cfg.HD ** -0.5, KEY_CHUNK=chunk)
    return pl.pallas_call(
        kern,
        grid=(B, T // QB),
        in_specs=[pl.BlockSpec((1, QB, iqw), lambda b, q: (b, q, oiq // iqw)),
                  pl.BlockSpec((1, QB, V7X_LANES), lambda b, q: (b, q, oiw // V7X_LANES)),
                  pl.BlockSpec((1, QB, aqw), lambda b, q: (b, q, oaq // aqw)),
                  pl.BlockSpec((1, T, V7X_LANES), lambda b, q: (b, 0, oik // V7X_LANES)),
                  pl.BlockSpec((1, T, kvw), lambda b, q: (b, 0, 0)),
                  pl.BlockSpec((1, T, kvw), lambda b, q: (b, 0, 0))],
        out_specs=pl.BlockSpec((1, QB, aqw), lambda b, q: (b, q, 0)),
        out_shape=jax.ShapeDtypeStruct((B, T, aqw), F32),
        scratch_shapes=[pltpu.VMEM((T // chunk, QB, chunk), F32),
                        pltpu.VMEM((T // chunk, G * QB, chunk), F32),
                        pltpu.VMEM((cfg.IH * QB, 2 * V7X_LANES), BF16),
                        pltpu.VMEM((cfg.IH * QB, 2 * V7X_LANES), F32)],
        compiler_params=_cparams(("arbitrary", "arbitrary")),
        name="dsa_prompt",
    )(proj, proj, proj, proj, kbf, vbf)


def _dsa_sample_select_kernel(pt_ref, q_ref, w_ref, kn_ref, kidx_hbm, idx_ref, kbuf, sc_ref, rank_ref, sel_ref, sem,
                              *, NP, NPP, PG, topk, wscale):
    b = pl.program_id(0)
    nb = pl.num_programs(0)
    slot = b % 2
    L = V7X_LANES

    def page_copy(sl, p, page):
        return pltpu.make_async_copy(kidx_hbm.at[0, page], kbuf.at[sl, p], sem.at[sl])

    def issue(bb, sl):
        def body(p, c):
            page_copy(sl, p, pt_ref[bb, p]).start()
            return c
        lax.fori_loop(0, NP, body, 0, unroll=math.gcd(NP, 2 * DMA_LOOP_UNROLL))

    @pl.when(b == 0)
    def _():
        issue(0, 0)

    @pl.when(b + 1 < nb)
    def _():
        issue(b + 1, 1 - slot)

    def wait_body(p, c):
        page_copy(slot, p, 0).wait()
        return c

    lax.fori_loop(0, NP, wait_body, 0, unroll=math.gcd(NP, 2 * DMA_LOOP_UNROLL))

    q = q_ref[0]
    w = w_ref[0][:, 0:1] * wscale
    qh, ql = _split2(q)
    q3 = jnp.concatenate([qh, ql, qh], axis=1)
    dn = (((1,), (1,)), ((), ()))

    def group_body(gi, c):
        kk = kbuf[slot, pl.ds(gi * PG, PG)]
        kk = jnp.concatenate([kk[j] for j in range(PG)], axis=1)
        kh, kl = _split2(kk)
        s = jnp.dot(q3, jnp.concatenate([kh, kh, kl], axis=0), preferred_element_type=F32)
        sc = jnp.sum(jnp.maximum(s, 0.0) * w, axis=0, keepdims=True)
        for j in range(PG):
            sc_ref[pl.ds(gi * PG + j, 1), :] = sc[:, j * L:(j + 1) * L]
        return c

    lax.fori_loop(0, NP // PG, group_body, 0)
    sn = jnp.sum(q * kn_ref[0], axis=1, keepdims=True)
    scn = jnp.sum(jnp.maximum(sn, 0.0) * w, axis=0, keepdims=True)
    tail = (NPP - NP, L)
    first = (lax.broadcasted_iota(I32, tail, 0) == 0) & (lax.broadcasted_iota(I32, tail, 1) == 0)
    sc_ref[NP:NPP, :] = jnp.where(first, scn, -jnp.inf)

    sc = sc_ref[...]
    pos = lax.broadcasted_iota(I32, sc.shape, 0) * L + lax.broadcasted_iota(I32, sc.shape, 1)

    def bit_body(i, t):
        cand = t | (jnp.int32(1) << (31 - i))
        hit = jnp.where(sc >= _offset_word_to_f32(cand), 1, 0)
        cnt = jnp.sum(jnp.sum(hit, axis=0, keepdims=True), axis=1, keepdims=True)
        return jnp.where(cnt >= topk, cand, t)

    thr0 = _offset_word_to_f32(lax.fori_loop(0, 32, bit_body, jnp.zeros((1, L), I32)))
    n_ge = jnp.sum(jnp.where(sc >= thr0, 1, 0))
    thr = jnp.min(jnp.where(sc >= thr0, sc, jnp.inf))
    cut = jnp.int32(INT_MAX)

    def selected(thr, cut):
        return (sc > thr) | ((sc == thr) & (pos <= cut))

    def drop_last(_, tc):
        thr, cut = tc
        sel = selected(thr, cut)
        low = jnp.min(jnp.where(sel, sc, jnp.inf))
        last = jnp.max(jnp.where(sel & (sc == low), pos, -1))
        return low, last - 1

    thr, cut = lax.fori_loop(0, n_ge - topk, drop_last, (thr, cut))
    sel = selected(thr, cut)

    self_ = jnp.where(sel, 1.0, 0.0)
    li = lax.broadcasted_iota(I32, (L, L), 0)
    lj = lax.broadcasted_iota(I32, (L, L), 1)
    before_lane = jnp.where(li < lj, 1.0, 0.0).astype(BF16)
    ri = lax.broadcasted_iota(I32, (NPP, NPP), 0)
    rj = lax.broadcasted_iota(I32, (NPP, NPP), 1)
    before_row = jnp.where(rj < ri, 1.0, 0.0).astype(BF16)
    rowtot = jnp.broadcast_to(jnp.sum(self_, axis=1, keepdims=True), (NPP, L))
    rank_ref[...] = (jnp.dot(self_.astype(BF16), before_lane, preferred_element_type=F32)
                     + jnp.dot(before_row, rowtot.astype(BF16), preferred_element_type=F32))
    sel_ref[...] = self_

    slot_id = lax.broadcasted_iota(I32, (topk, 1), 0).astype(F32)
    lane1 = lax.broadcasted_iota(I32, (1, L), 1)

    def place(r, acc):
        hit = (rank_ref[pl.ds(r, 1), :] == slot_id) & (sel_ref[pl.ds(r, 1), :] > 0.0)
        return acc + jnp.where(hit, (r * L + lane1).astype(F32), 0.0)

    placed = lax.fori_loop(0, NPP, place, jnp.zeros((topk, L), F32)).astype(I32)
    ones = jnp.ones((V7X_SUBLANES, L), BF16)
    lane_bits = L.bit_length() - 1
    hi = (placed >> lane_bits).astype(F32).astype(BF16)
    lo = (placed & (L - 1)).astype(F32).astype(BF16)
    idx = (lax.dot_general(ones, hi, dn, preferred_element_type=F32) * float(L)
           + lax.dot_general(ones, lo, dn, preferred_element_type=F32))
    idx_ref[0] = idx.astype(I32)


def _dsa_sample_attend_kernel(idx_s, pt_s, q_ref, idx_ref, kn_ref, vn_ref, ck_hbm, cv_hbm, o_ref, kbuf, vbuf, sem,
                              *, NKV, G, topk, past, page_shift, sm_scale):
    b = pl.program_id(0)
    nb = pl.num_programs(0)
    slot = b % 2
    page_mask = (1 << page_shift) - 1

    def row_copies(sl, j, page, r):
        return (pltpu.make_async_copy(ck_hbm.at[0, page, r], kbuf.at[sl, j], sem.at[0, sl]),
                pltpu.make_async_copy(cv_hbm.at[0, page, r], vbuf.at[sl, j], sem.at[1, sl]))

    def issue(bb, sl):
        def body(j, c):
            i = jnp.minimum(idx_s[bb, j], past - 1)
            for cp in row_copies(sl, j, pt_s[bb, i >> page_shift], i & page_mask):
                cp.start()
            return c
        lax.fori_loop(0, topk, body, 0, unroll=2 * DMA_LOOP_UNROLL)

    @pl.when(b == 0)
    def _():
        issue(0, 0)

    @pl.when(b + 1 < nb)
    def _():
        issue(b + 1, 1 - slot)

    def wait_body(j, c):
        for cp in row_copies(slot, j, 0, 0):
            cp.wait()
        return c

    lax.fori_loop(0, topk, wait_body, 0, unroll=2 * DMA_LOOP_UNROLL)

    idx = idx_ref[0][0:1, :]
    cached = idx < past
    new_sel = jnp.max(jnp.where(idx == past, 1, 0), axis=1, keepdims=True) > 0
    for g in range(NKV):
        qg = q_ref[0][g * G:(g + 1) * G, :]
        kg = kbuf[slot, :, g, :]
        vg = vbuf[slot, :, g, :]
        s = jnp.where(cached, _dot_nt(qg, kg) * sm_scale, -jnp.inf)
        sn = jnp.sum(qg * kn_ref[0][g:g + 1, :], axis=1, keepdims=True) * sm_scale
        sn = jnp.where(new_sel, sn, -jnp.inf)
        m = jnp.maximum(jnp.max(s, axis=1, keepdims=True), sn)
        p = jnp.exp(s - m)
        pn = jnp.exp(sn - m)
        l = jnp.sum(p, axis=1, keepdims=True) + pn
        o_ref[0, g * G:(g + 1) * G, :] = (_bdot(p, vg) + pn * vn_ref[0][g:g + 1, :]) / l


def dsa_sample(q_att, q_idx, w_idx, kn_idx, k_new, v_new, cache_k, cache_v, cache_kidx, page_table, cfg: Cfg):
    DB, NP, PAGE = cfg.DB, cfg.NPAGES, cfg.PAGE
    assert PAGE == V7X_LANES and cache_k.shape[0] == 1
    total = cfg.PAST + 1
    topk = min(TOPK_MAX, total // 4)
    assert topk <= total and topk % V7X_LANES == 0
    NPP = -(-(NP + 1) // V7X_SUBLANES) * V7X_SUBLANES
    PG = math.gcd(NP, 8)
    w_b = jnp.broadcast_to(w_idx[:, :, None], (DB, cfg.IH, V7X_LANES))
    sel_kern = functools.partial(_dsa_sample_select_kernel, NP=NP, NPP=NPP, PG=PG, topk=topk,
                                 wscale=(cfg.IH ** -0.5) * (cfg.ID ** -0.5))
    idx = pl.pallas_call(
        sel_kern,
        grid_spec=pltpu.PrefetchScalarGridSpec(
            num_scalar_prefetch=1,
            grid=(DB,),
            in_specs=[pl.BlockSpec((1, cfg.IH, cfg.ID), lambda b, pt: (b, 0, 0)),
                      pl.BlockSpec((1, cfg.IH, V7X_LANES), lambda b, pt: (b, 0, 0)),
                      pl.BlockSpec((1, 1, cfg.ID), lambda b, pt: (b, 0, 0)),
                      pl.BlockSpec(memory_space=pl.ANY)],
            out_specs=pl.BlockSpec((1, V7X_SUBLANES, topk), lambda b, pt: (b, 0, 0)),
            scratch_shapes=[pltpu.VMEM((2, NP, cfg.ID, PAGE), F32),
                            pltpu.VMEM((NPP, V7X_LANES), F32),
                            pltpu.VMEM((NPP, V7X_LANES), F32),
                            pltpu.VMEM((NPP, V7X_LANES), F32),
                            pltpu.SemaphoreType.DMA((2,))]),
        out_shape=jax.ShapeDtypeStruct((DB, V7X_SUBLANES, topk), I32),
        compiler_params=_cparams(("arbitrary",)),
        name="dsa_sample_select",
    )(page_table, q_idx, w_b, kn_idx.reshape(DB, 1, cfg.ID), jnp.swapaxes(cache_kidx, 2, 3))

    G = cfg.NH // cfg.NKV
    att_kern = functools.partial(_dsa_sample_attend_kernel, NKV=cfg.NKV, G=G, topk=topk, past=cfg.PAST,
                                 page_shift=PAGE.bit_length() - 1, sm_scale=cfg.HD ** -0.5)
    return pl.pallas_call(
        att_kern,
        grid_spec=pltpu.PrefetchScalarGridSpec(
            num_scalar_prefetch=2,
            grid=(DB,),
            in_specs=[pl.BlockSpec((1, cfg.NH, cfg.HD), lambda b, ix, pt: (b, 0, 0)),
                      pl.BlockSpec((1, V7X_SUBLANES, topk), lambda b, ix, pt: (b, 0, 0)),
                      pl.BlockSpec((1, cfg.NKV, cfg.HD), lambda b, ix, pt: (b, 0, 0)),
                      pl.BlockSpec((1, cfg.NKV, cfg.HD), lambda b, ix, pt: (b, 0, 0)),
                      pl.BlockSpec(memory_space=pl.ANY),
                      pl.BlockSpec(memory_space=pl.ANY)],
            out_specs=pl.BlockSpec((1, cfg.NH, cfg.HD), lambda b, ix, pt: (b, 0, 0)),
            scratch_shapes=[pltpu.VMEM((2, topk, cfg.NKV, cfg.HD), F32),
                            pltpu.VMEM((2, topk, cfg.NKV, cfg.HD), F32),
                            pltpu.SemaphoreType.DMA((2, 2))]),
        out_shape=jax.ShapeDtypeStruct((DB, cfg.NH, cfg.HD), F32),
        compiler_params=_cparams(("arbitrary",)),
        name="dsa_sample_attend",
    )(idx[:, 0, :], page_table, q_att, idx, k_new.reshape(DB, cfg.NKV, cfg.HD), v_new.reshape(DB, cfg.NKV, cfg.HD),
      cache_k, cache_v)


def _merge_kernel(og_ref, oa_ref, wg_ref, wa_ref, za_ref, zb_ref, o_ref, ogb_ref, oab_ref):
    @pl.when(pl.program_id(1) == 0)
    def _():
        ogb_ref[...] = og_ref[...].astype(BF16)
        oab_ref[...] = oa_ref[...].astype(BF16)

    a = jnp.dot(ogb_ref[...], wg_ref[...].astype(BF16), preferred_element_type=F32)
    b = jnp.dot(oab_ref[...], wa_ref[...].astype(BF16), preferred_element_type=F32)
    o_ref[...] = za_ref[...] * a + zb_ref[...] * b


def merge_branches(og, oa, w_gla_o, w_attn_o, proj, cfg: Cfg, lay: ProjLayout, tm):
    N = og.shape[0]
    D, tn = cfg.D, lay.tn
    oza, ozb = lay.offs["za"] // tn, lay.offs["zb"] // tn
    return pl.pallas_call(
        _merge_kernel,
        grid=(N // tm, D // tn),
        in_specs=[pl.BlockSpec((tm, og.shape[1]), lambda i, j: (i, 0)),
                  pl.BlockSpec((tm, oa.shape[1]), lambda i, j: (i, 0)),
                  pl.BlockSpec((og.shape[1], tn), lambda i, j: (0, j)),
                  pl.BlockSpec((oa.shape[1], tn), lambda i, j: (0, j)),
                  pl.BlockSpec((tm, tn), lambda i, j: (i, oza + j)),
                  pl.BlockSpec((tm, tn), lambda i, j: (i, ozb + j))],
        out_specs=pl.BlockSpec((tm, tn), lambda i, j: (i, j)),
        out_shape=jax.ShapeDtypeStruct((N, D), F32),
        scratch_shapes=[pltpu.VMEM((tm, og.shape[1]), BF16), pltpu.VMEM((tm, oa.shape[1]), BF16)],
        compiler_params=_cparams(("arbitrary", "arbitrary")),
        name="merge_branches",
    )(og, oa, w_gla_o, w_attn_o, proj, proj)


def _outproj_kernel(m_ref, w_ref, x_ref, gt_ref, o_ref, mb_ref):
    @pl.when(pl.program_id(1) == 0)
    def _():
        mb_ref[...] = m_ref[...].astype(BF16)

    o_ref[...] = x_ref[...] + gt_ref[...] * jnp.dot(mb_ref[...], w_ref[...].astype(BF16), preferred_element_type=F32)


def out_proj(merged, w_out, x, gt, tm, rows_per_group, tn):
    N, D = x.shape
    G, R, _ = gt.shape
    tpg = rows_per_group // tm
    return pl.pallas_call(
        _outproj_kernel,
        grid=(N // tm, D // tn),
        in_specs=[pl.BlockSpec((tm, D), lambda i, j: (i, 0)),
                  pl.BlockSpec((D, tn), lambda i, j: (0, j)),
                  pl.BlockSpec((tm, tn), lambda i, j: (i, j)),
                  pl.BlockSpec((None, R, tn), lambda i, j: (i // tpg, 0, j))],
        out_specs=pl.BlockSpec((tm, tn), lambda i, j: (i, j)),
        out_shape=jax.ShapeDtypeStruct((N, D), F32),
        scratch_shapes=[pltpu.VMEM((tm, D), BF16)],
        compiler_params=_cparams(("arbitrary", "arbitrary")),
        name="out_proj",
    )(merged, w_out, x, gt)


def _ffn_pre_kernel(x_ref, g_ref, sc_ref, sh_ref, wrt_ref, br_ref, wg_ref, wu_ref, wd_ref,
                    h2_ref, ysh_ref, eidx_ref, wts_ref,
                    wrh_ref, wrl_ref, wgb_ref, wub_ref, wdb_ref, *, E, NG, first_tile):
    @pl.when(pl.program_id(0) == first_tile)
    def _():
        hi, lo = _split2(wrt_ref[...])
        wrh_ref[...] = hi
        wrl_ref[...] = lo
        wgb_ref[...] = wg_ref[...].astype(BF16)
        wub_ref[...] = wu_ref[...].astype(BF16)
        wdb_ref[...] = wd_ref[...].astype(BF16)

    x = x_ref[...]
    tm = x.shape[0]
    ms = jnp.mean(x * x, axis=-1, keepdims=True)
    h2 = x * lax.rsqrt(ms + NORM_EPS) * g_ref[...] * (1.0 + sc_ref[...]) + sh_ref[...]
    h2_ref[...] = h2
    hh, hl = _split2(h2)

    gate = jnp.dot(hh, wgb_ref[...], preferred_element_type=F32)
    up = jnp.dot(hh, wub_ref[...], preferred_element_type=F32)
    ysh_ref[...] = jnp.dot((_silu(gate) * up).astype(BF16), wdb_ref[...], preferred_element_type=F32)

    dn = (((1,), (1,)), ((), ()))
    logit = (lax.dot_general(wrh_ref[...], hh, dn, preferred_element_type=F32)
             + lax.dot_general(wrh_ref[...], hl, dn, preferred_element_type=F32)
             + lax.dot_general(wrl_ref[...], hh, dn, preferred_element_type=F32))
    s = _sigmoid(logit)
    PG = E // NG
    s3 = s.reshape(NG, PG, tm)
    sel3 = (s + br_ref[:, 0:1]).reshape(NG, PG, tm)
    ninf = -jnp.inf
    ipg = lax.broadcasted_iota(I32, (NG, PG, tm), 1)
    m1 = jnp.max(sel3, axis=1, keepdims=True)
    i1 = jnp.min(jnp.where(sel3 == m1, ipg, PG), axis=1, keepdims=True)
    m2 = jnp.max(jnp.where(ipg == i1, ninf, sel3), axis=1, keepdims=True)
    gs = (m1 + m2).reshape(NG, tm)
    ig = lax.broadcasted_iota(I32, (NG, tm), 0)
    gmask = jnp.zeros((NG, tm), jnp.bool_)
    for _ in range(TOPK_GROUPS):
        m = jnp.max(gs, axis=0, keepdims=True)
        i = jnp.min(jnp.where(gs == m, ig, NG), axis=0, keepdims=True)
        pick = ig == i
        gmask = gmask | pick
        gs = jnp.where(pick, ninf, gs)
    cand = jnp.where(gmask.reshape(NG, 1, tm), sel3, ninf)
    ie = lax.broadcasted_iota(I32, (NG, PG, tm), 0) * PG + ipg
    idxs, ws = [], []
    for _ in range(TOP_K):
        m = jnp.max(jnp.max(cand, axis=0, keepdims=True), axis=1, keepdims=True)
        hit = jnp.where(cand == m, ie, E)
        i = jnp.min(jnp.min(hit, axis=0, keepdims=True), axis=1, keepdims=True)
        pick = ie == i
        wsel = jnp.where(pick, s3, 0.0)
        ws.append(jnp.sum(jnp.sum(wsel, axis=0, keepdims=True), axis=1, keepdims=True).reshape(1, tm))
        idxs.append(i.reshape(1, tm))
        cand = jnp.where(pick, ninf, cand)
    tot = ws[0]
    for w in ws[1:]:
        tot = tot + w
    for k in range(TOP_K):
        eidx_ref[k:k + 1, :] = idxs[k]
        wts_ref[k:k + 1, :] = ws[k] / tot * ROUTED_SCALE


def ffn_pre(x1, g, sc, sh, w_router_t, b_router, w_sh_gate, w_sh_up, w_sh_down, cfg: Cfg, tm):
    N, D = x1.shape
    G, R, _ = sc.shape
    tpg = (N // G) // tm
    E, DE = cfg.E, cfg.DE
    assert E // N_GROUPS == V7X_SUBLANES
    mod_spec = pl.BlockSpec((None, R, D), lambda i: (i // tpg, 0, 0))
    const = lambda shape: pl.BlockSpec(shape, lambda i: (0,) * len(shape))
    return pl.pallas_call(
        functools.partial(_ffn_pre_kernel, E=E, NG=N_GROUPS, first_tile=0),
        grid=(N // tm,),
        in_specs=[pl.BlockSpec((tm, D), lambda i: (i, 0)), const((1, D)), mod_spec, mod_spec,
                  const((E, D)), const((E, V7X_LANES)), const((D, DE)), const((D, DE)), const((DE, D))],
        out_specs=[pl.BlockSpec((tm, D), lambda i: (i, 0)),
                   pl.BlockSpec((tm, D), lambda i: (i, 0)),
                   pl.BlockSpec((TOP_K, tm), lambda i: (0, i)),
                   pl.BlockSpec((TOP_K, tm), lambda i: (0, i))],
        out_shape=[jax.ShapeDtypeStruct((N, D), F32),
                   jax.ShapeDtypeStruct((N, D), F32),
                   jax.ShapeDtypeStruct((TOP_K, N), I32),
                   jax.ShapeDtypeStruct((TOP_K, N), F32)],
        scratch_shapes=[pltpu.VMEM((E, D), BF16), pltpu.VMEM((E, D), BF16),
                        pltpu.VMEM((D, DE), BF16), pltpu.VMEM((D, DE), BF16), pltpu.VMEM((DE, D), BF16)],
        compiler_params=_cparams(("arbitrary",)),
        name="ffn_pre",
    )(x1, g.reshape(1, D), sc, sh, w_router_t, jnp.broadcast_to(b_router[:, None], (E, V7X_LANES)),
      w_sh_gate, w_sh_up, w_sh_down)


def _rank_kernel(eidx_ref, rank_ref, cnt_ref, carry_ref, *, E):
    i = pl.program_id(0)

    @pl.when(i == 0)
    def _():
        carry_ref[...] = jnp.zeros(carry_ref.shape, F32)

    eidx = eidx_ref[...]
    tm = eidx.shape[1]
    eio = lax.broadcasted_iota(I32, (E, 1), 0)
    onehot = jnp.zeros((E, tm), F32)
    for k in range(TOP_K):
        onehot = onehot + jnp.where(eidx[k:k + 1, :] == eio, 1.0, 0.0)
    r = lax.broadcasted_iota(I32, (tm, tm), 0)
    c = lax.broadcasted_iota(I32, (tm, tm), 1)
    tri = jnp.where(r <= c, 1.0, 0.0).astype(BF16)
    cum = jnp.dot(onehot.astype(BF16), tri, preferred_element_type=F32)
    excl = cum - onehot + carry_ref[:, 0:1]
    for k in range(TOP_K):
        hit = eidx[k:k + 1, :] == eio
        rank_ref[k:k + 1, :] = jnp.sum(jnp.where(hit, excl, 0.0), axis=0, keepdims=True).astype(I32)
    carry_ref[...] = carry_ref[...] + cum[:, tm - 1:tm]

    @pl.when(i == pl.num_programs(0) - 1)
    def _():
        cnt_ref[...] = carry_ref[...].astype(I32)


def _plan_kernel(eidx_ref, rank_ref, cnt_ref, dest_ref, be_ref, nr_ref, *, E, bm, NBP):
    L = V7X_LANES
    sh = bm.bit_length() - 1
    cnt = cnt_ref[...]
    padded = ((cnt + (bm - 1)) >> sh) << sh
    r = lax.broadcasted_iota(I32, (E, E), 0)
    c = lax.broadcasted_iota(I32, (E, E), 1)
    low = jnp.where(c < r, 1.0, 0.0).astype(BF16)
    pstart = jnp.zeros((E, L), F32)
    for shift in (0, 8, 16):
        piece = ((padded >> shift) & 255).astype(F32).astype(BF16)
        pstart = pstart + jnp.dot(low, piece, preferred_element_type=F32) * float(1 << shift)
    pstart = pstart.astype(I32)
    pend = pstart + padded
    eio = lax.broadcasted_iota(I32, (E, 1), 0)
    eidx = eidx_ref[...]
    ps_col = pstart[:, 0:1]
    for k in range(TOP_K):
        e = eidx[k:k + 1, :]
        base = jnp.sum(jnp.where(e == eio, ps_col, 0), axis=0, keepdims=True)
        dest_ref[k:k + 1, :] = jnp.where(e >= 0, base + rank_ref[k:k + 1, :], -1)

    @pl.when(pl.program_id(0) == 0)
    def _():
        blk0 = lax.broadcasted_iota(I32, (1, NBP), 1) * bm
        be = jnp.sum((pend[:, 0:1] <= blk0).astype(I32), axis=0, keepdims=True)
        be = jnp.minimum(be, E - 1)
        last = jnp.sum(jnp.where(eio == be, (pstart + cnt)[:, 0:1], 0), axis=0, keepdims=True)
        be_ref[...] = be
        nr_ref[...] = jnp.clip(last - blk0, 0, bm)


def moe_plan(eidx, cfg: Cfg, bm, tm):
    K, N = eidx.shape
    E = cfg.E
    NB = -(-(N * K) // bm) + E
    NBP = -(-NB // V7X_LANES) * V7X_LANES
    rank, cnt = pl.pallas_call(
        functools.partial(_rank_kernel, E=E),
        grid=(N // tm,),
        in_specs=[pl.BlockSpec((K, tm), lambda i: (0, i))],
        out_specs=[pl.BlockSpec((K, tm), lambda i: (0, i)), pl.BlockSpec((E, V7X_LANES), lambda i: (0, 0))],
        out_shape=[jax.ShapeDtypeStruct((K, N), I32), jax.ShapeDtypeStruct((E, V7X_LANES), I32)],
        scratch_shapes=[pltpu.VMEM((E, V7X_LANES), F32)],
        compiler_params=_cparams(("arbitrary",)),
        name="moe_rank",
    )(eidx)
    dest, be, nr = pl.pallas_call(
        functools.partial(_plan_kernel, E=E, bm=bm, NBP=NBP),
        grid=(N // tm,),
        in_specs=[pl.BlockSpec((K, tm), lambda i: (0, i)), pl.BlockSpec((K, tm), lambda i: (0, i)),
                  pl.BlockSpec((E, V7X_LANES), lambda i: (0, 0))],
        out_specs=[pl.BlockSpec((K, tm), lambda i: (0, i)), pl.BlockSpec((1, NBP), lambda i: (0, 0)),
                   pl.BlockSpec((1, NBP), lambda i: (0, 0))],
        out_shape=[jax.ShapeDtypeStruct((K, N), I32), jax.ShapeDtypeStruct((1, NBP), I32),
                   jax.ShapeDtypeStruct((1, NBP), I32)],
        compiler_params=_cparams(("arbitrary",)),
        name="moe_plan",
    )(eidx, rank, cnt)
    return dest, be.reshape(NBP), nr.reshape(NBP), NB


def _dispatch_kernel(dest_ref, hp_ref, hs_ref, xs_ref, sem, *, tm, tiles_p):
    i = pl.program_id(0)

    def scatter_rows(h_ref, rows):
        def row_copy(t, k):
            return pltpu.make_async_copy(h_ref.at[pl.ds(t, 1)], xs_ref.at[pl.ds(dest_ref[k, t], 1)], sem)

        def start(t, c):
            for k in range(TOP_K):
                row_copy(t, k).start()
            return c

        def wait(t, c):
            for k in range(TOP_K):
                row_copy(t, k).wait()
            return c

        lax.fori_loop(0, rows, start, 0, unroll=DMA_LOOP_UNROLL)
        lax.fori_loop(0, rows, wait, 0, unroll=DMA_LOOP_UNROLL)

    @pl.when(i < tiles_p)
    def _():
        scatter_rows(hp_ref, tm)

    @pl.when(i >= tiles_p)
    def _():
        scatter_rows(hs_ref, hs_ref.shape[0])


def moe_dispatch(h2_p, h2_s, dest, P, tm):
    D = h2_p.shape[1]
    NS = h2_s.shape[0]
    assert h2_p.shape[0] % tm == 0 and NS <= tm
    tiles_p = h2_p.shape[0] // tm
    assert dest.shape[1] == (tiles_p + 1) * tm
    return pl.pallas_call(
        functools.partial(_dispatch_kernel, tm=tm, tiles_p=tiles_p),
        grid=(tiles_p + 1,),
        in_specs=[pl.BlockSpec((TOP_K, tm), lambda i: (0, i), memory_space=pltpu.SMEM),
                  pl.BlockSpec((tm, D), lambda i: (jnp.minimum(i, tiles_p - 1), 0)),
                  pl.BlockSpec((NS, D), lambda i: (0, 0))],
        out_specs=pl.BlockSpec(memory_space=pl.ANY),
        out_shape=jax.ShapeDtypeStruct((P, D), F32),
        scratch_shapes=[pltpu.SemaphoreType.DMA(())],
        compiler_params=_cparams(("arbitrary",)),
        name="moe_dispatch",
    )(dest, h2_p, h2_s)


def _experts_kernel(be_ref, nr_ref, x_ref, wg_ref, wu_ref, wd_ref, y_ref, wgb_ref, wub_ref, wdb_ref):
    b = pl.program_id(0)
    prev = be_ref[jnp.maximum(b - 1, 0)]
    n = nr_ref[b]

    @pl.when((n > 0) & ((b == 0) | (be_ref[b] != prev)))
    def _():
        wgb_ref[...] = wg_ref[0].astype(BF16)
        wub_ref[...] = wu_ref[0].astype(BF16)
        wdb_ref[...] = wd_ref[0].astype(BF16)

    @pl.when(n > 0)
    def _():
        row = lax.broadcasted_iota(I32, (x_ref.shape[0], 1), 0)
        x = jnp.where(row < n, x_ref[...], 0.0).astype(BF16)
        gate = jnp.dot(x, wgb_ref[...], preferred_element_type=F32)
        up = jnp.dot(x, wub_ref[...], preferred_element_type=F32)
        y_ref[...] = jnp.dot((_silu(gate) * up).astype(BF16), wdb_ref[...], preferred_element_type=F32)

    @pl.when(n == 0)
    def _():
        y_ref[...] = jnp.zeros(y_ref.shape, F32)


def moe_experts(xs, be, nr, w_gate, w_up, w_down, NB, bm):
    P, D = xs.shape
    E, _, DE = w_gate.shape
    return pl.pallas_call(
        _experts_kernel,
        grid_spec=pltpu.PrefetchScalarGridSpec(
            num_scalar_prefetch=2,
            grid=(NB,),
            in_specs=[pl.BlockSpec((bm, D), lambda b, be, nr: (b, 0)),
                      pl.BlockSpec((1, D, DE), lambda b, be, nr: (be[b], 0, 0)),
                      pl.BlockSpec((1, D, DE), lambda b, be, nr: (be[b], 0, 0)),
                      pl.BlockSpec((1, DE, D), lambda b, be, nr: (be[b], 0, 0))],
            out_specs=pl.BlockSpec((bm, D), lambda b, be, nr: (b, 0)),
            scratch_shapes=[pltpu.VMEM((D, DE), BF16), pltpu.VMEM((D, DE), BF16), pltpu.VMEM((DE, D), BF16)]),
        out_shape=jax.ShapeDtypeStruct((P, D), F32),
        compiler_params=_cparams(("arbitrary",)),
        name="moe_experts",
    )(be, nr, xs, w_gate, w_up, w_down)


def _combine_kernel(dest_ref, ys_ref, wts_ref, x_ref, ysh_ref, gt_ref, gf_ref, o_ref, ybuf_ref, sem, *, tm):
    def row_copy(t, k):
        return pltpu.make_async_copy(ys_ref.at[pl.ds(dest_ref[k, t], 1)], ybuf_ref.at[k, pl.ds(t, 1)], sem)

    def start(t, c):
        for k in range(TOP_K):
            row_copy(t, k).start()
        return c

    def wait(t, c):
        for k in range(TOP_K):
            row_copy(t, k).wait()
        return c

    lax.fori_loop(0, tm, start, 0, unroll=DMA_LOOP_UNROLL)
    lax.fori_loop(0, tm, wait, 0, unroll=DMA_LOOP_UNROLL)
    w = wts_ref[...]
    ffn = ysh_ref[...]
    for k in range(TOP_K):
        ffn = ffn + w[:, k:k + 1] * ybuf_ref[k]
    x2 = x_ref[...] + gt_ref[...] * ffn
    ms = jnp.mean(x2 * x2, axis=-1, keepdims=True)
    o_ref[...] = x2 * lax.rsqrt(ms + NORM_EPS) * gf_ref[...]


def moe_combine(ys, dest, wts_t, x1, ysh, gt, g_final, tm, rows_per_group):
    N, D = x1.shape
    G, R, _ = gt.shape
    tpg = rows_per_group // tm
    return pl.pallas_call(
        functools.partial(_combine_kernel, tm=tm),
        grid=(N // tm,),
        in_specs=[pl.BlockSpec((TOP_K, tm), lambda i: (0, i), memory_space=pltpu.SMEM),
                  pl.BlockSpec(memory_space=pl.ANY),
                  pl.BlockSpec((tm, TOP_K), lambda i: (i, 0)),
                  pl.BlockSpec((tm, D), lambda i: (i, 0)),
                  pl.BlockSpec((tm, D), lambda i: (i, 0)),
                  pl.BlockSpec((None, R, D), lambda i: (i // tpg, 0, 0)),
                  pl.BlockSpec((1, D), lambda i: (0, 0))],
        out_specs=pl.BlockSpec((tm, D), lambda i: (i, 0)),
        out_shape=jax.ShapeDtypeStruct((N, D), F32),
        scratch_shapes=[pltpu.VMEM((TOP_K, tm, D), F32), pltpu.SemaphoreType.DMA(())],
        compiler_params=_cparams(("arbitrary",)),
        name="moe_combine",
    )(dest, ys, wts_t, x1, ysh, gt, g_final.reshape(1, D))


def make_cfg(x_prompt, x_sample, cache_k, cache_kidx, state_gla, page_table, w_in, w_gla_a, w_router, w_exp_gate):
    B, T, D = x_prompt.shape
    DB = x_sample.shape[0]
    assert x_sample.shape[1] == 1 and w_in.shape[0] == 1
    _, _, PAGE, NKV, HD = cache_k.shape
    ID = cache_kidx.shape[-1]
    _, _, GH, DK, DV = state_gla.shape
    RANK = w_gla_a.shape[1]
    NH = D // HD
    rest = w_in.shape[2] - (2 * GH * DK + 2 * GH * DV + RANK + NH * HD + 2 * NKV * HD + ID + 2 * D)
    IH = rest // (ID + 1)
    assert IH * (ID + 1) == rest
    NPAGES = page_table.shape[1]
    return Cfg(D=D, B=B, T=T, DB=DB, GH=GH, DK=DK, DV=DV, RANK=RANK, NH=NH, NKV=NKV, HD=HD, IH=IH, ID=ID,
               E=w_router.shape[2], DE=w_exp_gate.shape[3], PAGE=PAGE, NPAGES=NPAGES, PAST=NPAGES * PAGE)


def _row_tile(n, cap):
    t = cap
    while n % t:
        t //= 2
    return t


def kernel(x_prompt, x_sample, c_prompt, c_sample, cache_k, cache_v, cache_kidx, state_gla, page_table, w_ada, b_ada,
           g_norm1, w_in, w_gla_a, b_gla_a, g_gla_norm, w_gla_o, w_attn_o, w_out, g_norm2, w_router, b_router,
           w_sh_gate, w_sh_up, w_sh_down, w_exp_gate, w_exp_up, w_exp_down, g_final):
    cfg = make_cfg(x_prompt, x_sample, cache_k, cache_kidx, state_gla, page_table, w_in, w_gla_a, w_router, w_exp_gate)
    lay = proj_layout(cfg)
    B, T, D, DB = cfg.B, cfg.T, cfg.D, cfg.DB
    NP_, NS = B * T, DB
    kvw = cfg.NKV * cfg.HD

    pad = (-(B + DB)) % V7X_SUBLANES
    c_all = jnp.concatenate([c_prompt, c_sample, jnp.zeros((pad, D), F32)], axis=0)
    mod = ada_mod(c_all, w_ada[0], b_ada[0])
    sh1p, sc1p, gt1p, sh2p, sc2p, gt2p = [m[:, None, :] for m in jnp.split(mod[:B], 6, axis=-1)]
    sh1s, sc1s, gt1s, sh2s, sc2s, gt2s = [m[None] for m in jnp.split(mod[B:B + DB], 6, axis=-1)]

    w_pad = w_in_views(w_in[0], cfg, lay)
    tm_p = _row_tile(T, 1024)
    xp = x_prompt.reshape(NP_, D)
    xs = x_sample.reshape(NS, D)
    proj_p = in_proj(xp, g_norm1[0], sc1p, sh1p, w_pad, rope_tables(np.arange(T), cfg), cfg, lay, tm_p, T)
    proj_s = in_proj(xs, g_norm1[0], sc1s, sh1s, w_pad, rope_tables(np.full((NS,), cfg.PAST), cfg), cfg, lay, NS, NS)

    def grp(proj, n, w=None):
        return proj[:, lay.offs[n]:lay.offs[n] + (lay.widths[n] if w is None else w)]

    proj_p3 = proj_p.reshape(B, T, lay.total)
    og_p, gla_p = gla_prompt(proj_p3, w_gla_a[0], b_gla_a[0], g_gla_norm[0], cfg, lay)
    og_s, gla_s = gla_sample(proj_s.reshape(DB, 1, lay.total), state_gla[0], w_gla_a[0], b_gla_a[0], g_gla_norm[0],
                             cfg, lay)

    k_p, v_p, kidx_p = grp(proj_p, "ak"), grp(proj_p, "av"), grp(proj_p, "ik", cfg.ID)
    k_s, v_s, kidx_s = grp(proj_s, "ak"), grp(proj_s, "av"), grp(proj_s, "ik", cfg.ID)
    oa_p = dsa_prompt(proj_p3, k_p.astype(BF16).reshape(B, T, kvw), v_p.astype(BF16).reshape(B, T, kvw), cfg, lay)
    oa_s = dsa_sample(grp(proj_s, "aq").reshape(DB, cfg.NH, cfg.HD), grp(proj_s, "iq").reshape(DB, cfg.IH, cfg.ID),
                      grp(proj_s, "iw", cfg.IH), kidx_s, k_s, v_s, cache_k, cache_v, cache_kidx, page_table, cfg)

    wgo, wao, wo = w_gla_o[0].astype(BF16), w_attn_o[0].astype(BF16), w_out[0].astype(BF16)
    mg_p = merge_branches(og_p.reshape(NP_, -1), oa_p.reshape(NP_, -1), wgo, wao, proj_p, cfg, lay, _row_tile(T, 512))
    mg_s = merge_branches(og_s.reshape(NS, -1), oa_s.reshape(NS, -1), wgo, wao, proj_s, cfg, lay, NS)
    x1_p = out_proj(mg_p, wo, xp, gt1p, _row_tile(T, 1024), T, lay.tn)
    x1_s = out_proj(mg_s, wo, xs, gt1s, NS, NS, lay.tn)

    tile = V7X_LANES
    assert NP_ % tile == 0 and NS <= tile
    n_tok = NP_ + tile
    wr_t = w_router[0].T
    tm_f = _row_tile(T, 256)
    h2_p, ysh_p, eidx_p, wts_p = ffn_pre(x1_p, g_norm2[0], sc2p, sh2p, wr_t, b_router[0], w_sh_gate[0], w_sh_up[0],
                                         w_sh_down[0], cfg, tm_f)
    h2_s, ysh_s, eidx_s, wts_s = ffn_pre(x1_s, g_norm2[0], sc2s, sh2s, wr_t, b_router[0], w_sh_gate[0], w_sh_up[0],
                                         w_sh_down[0], cfg, NS)
    eidx = jnp.concatenate([eidx_p, eidx_s, jnp.full((TOP_K, n_tok - NP_ - NS), -1, I32)], axis=1)
    bm = 256 if NP_ * TOP_K >= 256 * cfg.E * 4 else 128
    dest, be, nr, NB = moe_plan(eidx, cfg, bm, tile)
    xs_sorted = moe_dispatch(h2_p, h2_s, dest, NB * bm, tile)
    ys_sorted = moe_experts(xs_sorted, be, nr, w_exp_gate[0], w_exp_up[0], w_exp_down[0], NB, bm)
    tm_c = _row_tile(T, 128)
    y_p = moe_combine(ys_sorted, dest[:, :NP_], wts_p.T, x1_p, ysh_p, gt2p, g_final, tm_c, T)
    y_s = moe_combine(ys_sorted, dest[:, NP_:NP_ + NS], wts_s.T, x1_s, ysh_s, gt2s, g_final, NS, NS)

    return (y_p.reshape(B, T, D), y_s.reshape(DB, 1, D),
            k_p.reshape(1, B, T, cfg.NKV, cfg.HD), v_p.reshape(1, B, T, cfg.NKV, cfg.HD), kidx_p.reshape(1, B, T, cfg.ID),
            gla_p[None],
            k_s.reshape(1, DB, 1, cfg.NKV, cfg.HD), v_s.reshape(1, DB, 1, cfg.NKV, cfg.HD), kidx_s.reshape(1, DB, 1, cfg.ID),
            gla_s[None])
```

```python
import functools
import math
from typing import NamedTuple

import jax
import jax.numpy as jnp
import numpy as np
from jax import lax
from jax.experimental import pallas as pl
from jax.experimental.pallas import tpu as pltpu

F32 = jnp.float32
BF16 = jnp.bfloat16
I32 = jnp.int32

V7X_LANES = 128
V7X_SUBLANES = 8
BF16_ROW_TILE = 16
V7X_VMEM_LIMIT_BYTES = 56 * 1024 * 1024

GLA_TAU = 16.0
GLA_CHUNK = 64
ROPE_THETA = 500000.0
TOPK_MAX = 256
Q_BLOCK = 128
KEY_CHUNK = 1024
DMA_LOOP_UNROLL = 4
N_GROUPS = 8
TOPK_GROUPS = 4
TOP_K = 8
ROUTED_SCALE = 2.5
NORM_EPS = 1e-6
INT_MIN = -(2 ** 31)
INT_MAX = 2 ** 31 - 1
NEG_BIG = -1e30


class Cfg(NamedTuple):
    D: int
    B: int
    T: int
    DB: int
    GH: int
    DK: int
    DV: int
    RANK: int
    NH: int
    NKV: int
    HD: int
    IH: int
    ID: int
    E: int
    DE: int
    PAGE: int
    NPAGES: int
    PAST: int


def _cparams(sem, vmem=V7X_VMEM_LIMIT_BYTES):
    return pltpu.CompilerParams(dimension_semantics=sem, vmem_limit_bytes=vmem)


def _bdot(a, b):
    return jnp.dot(a.astype(BF16), b.astype(BF16), preferred_element_type=F32)


def _dot_nt(a, b):
    return lax.dot_general(a.astype(BF16), b.astype(BF16), (((1,), (1,)), ((), ())),
                           preferred_element_type=F32)


def _dot_tn(a, b):
    return lax.dot_general(a.astype(BF16), b.astype(BF16), (((0,), (0,)), ((), ())),
                           preferred_element_type=F32)


def _split2(x):
    hi = x.astype(BF16)
    lo = (x - hi.astype(F32)).astype(BF16)
    return hi, lo


def _split3(x):
    hi = x.astype(BF16)
    r1 = x - hi.astype(F32)
    mid = r1.astype(BF16)
    lo = (r1 - mid.astype(F32)).astype(BF16)
    return hi, mid, lo


def _sigmoid(x):
    return 1.0 / (1.0 + jnp.exp(-x))


def _silu(x):
    return x * _sigmoid(x)


def _ada_kernel(c_ref, w_ref, b_ref, o_ref):
    c = c_ref[...]
    o_ref[...] = _bdot(_silu(c), w_ref[...]) + b_ref[...]


def ada_mod(c, w_ada, b_ada):
    R, D = c.shape
    N = w_ada.shape[1]
    tn = 512 if N % 512 == 0 else 128
    return pl.pallas_call(
        _ada_kernel,
        grid=(N // tn,),
        in_specs=[pl.BlockSpec((R, D), lambda j: (0, 0)),
                  pl.BlockSpec((D, tn), lambda j: (0, j)),
                  pl.BlockSpec((1, tn), lambda j: (0, j))],
        out_specs=pl.BlockSpec((R, tn), lambda j: (0, j)),
        out_shape=jax.ShapeDtypeStruct((R, N), F32),
        compiler_params=_cparams(("arbitrary",)),
        name="ada_mod",
    )(c, w_ada, b_ada.reshape(1, N))


class ProjLayout(NamedTuple):
    names: tuple
    offs: dict
    widths: dict
    total: int
    tn: int
    chunk_types: tuple


def proj_layout(cfg: Cfg) -> ProjLayout:
    qk = cfg.GH * cfg.DK
    vv = cfg.GH * cfg.DV
    aq = cfg.NH * cfg.HD
    kv = cfg.NKV * cfg.HD
    iq = cfg.IH * cfg.ID
    big = [("gq", qk), ("gk", qk), ("gv", vv), ("gr", vv), ("aq", aq), ("ak", kv), ("av", kv),
           ("iq", iq), ("za", cfg.D), ("zb", cfg.D)]
    tn = 512
    for _, w in big:
        tn = math.gcd(tn, w)
    assert tn % V7X_LANES == 0
    offs, widths, types = {}, {}, []
    off = 0
    for n, w in big:
        offs[n], widths[n] = off, w
        t = {"aq": 1, "ak": 1, "iq": 2, "za": 3, "zb": 3}.get(n, 0)
        types += [t] * (w // V7X_LANES)
        off += w
    for n, t in (("ga", 0), ("ik", 2), ("iw", 0), ("pad", 0)):
        offs[n], widths[n] = off, V7X_LANES
        types.append(t)
        off += V7X_LANES
    assert off % tn == 0
    return ProjLayout(tuple(n for n, _ in big) + ("ga", "ik", "iw", "pad"), offs, widths, off, tn, tuple(types))


def w_in_views(w_in, cfg: Cfg, lay: ProjLayout):
    qk = cfg.GH * cfg.DK
    vv = cfg.GH * cfg.DV
    aq = cfg.NH * cfg.HD
    kv = cfg.NKV * cfg.HD
    iq = cfg.IH * cfg.ID
    src_w = dict(gq=qk, gk=qk, gv=vv, ga=cfg.RANK, gr=vv, aq=aq, ak=kv, av=kv, iq=iq, ik=cfg.ID, iw=cfg.IH,
                 za=cfg.D, zb=cfg.D)
    src_off, o = {}, 0
    for n in ("gq", "gk", "gv", "ga", "gr", "aq", "ak", "av", "iq", "ik", "iw", "za", "zb"):
        src_off[n] = o
        o += src_w[n]
    assert o == w_in.shape[1]
    D = w_in.shape[0]
    wt = jnp.swapaxes(w_in, 0, 1).astype(BF16)

    def rows(n):
        return wt[src_off[n]:src_off[n] + src_w[n]]

    def padded(a):
        return jnp.pad(a, ((0, V7X_LANES - a.shape[0]), (0, 0)))

    assert 2 * cfg.ID == V7X_LANES and cfg.RANK <= V7X_LANES and cfg.IH <= V7X_LANES
    small = jnp.concatenate([padded(rows("ga")), rows("ik"), rows("ik"), padded(rows("iw")),
                             jnp.zeros((V7X_LANES, D), BF16)], axis=0)
    tile_src = []
    for n in ("gq", "gk", "gv", "gr", "aq", "ak", "av", "iq", "za", "zb"):
        assert src_off[n] % BF16_ROW_TILE == 0 and lay.widths[n] == src_w[n]
        tile_src += [src_off[n] + t * lay.tn for t in range(src_w[n] // lay.tn)]
    return wt, small, np.asarray(tile_src, np.int32)


def rope_tables(pos, cfg: Cfg):
    pos = np.asarray(pos, np.float64)

    def one(width, rot):
        half = rot // 2
        inv = ROPE_THETA ** (-np.arange(half, dtype=np.float64) * 2.0 / rot)
        ang = pos[:, None] * inv[None, :]
        lane = np.arange(V7X_LANES) % width
        c = np.where(lane < rot, np.cos(ang)[:, lane % half], 1.0)
        s1 = np.where((lane >= half) & (lane < rot), np.sin(ang)[:, (lane - half) % half], 0.0)
        s2 = np.where(lane < half, -np.sin(ang)[:, lane % half], 0.0)
        return [c, s1, s2]

    tabs = one(cfg.HD, cfg.HD // 4) + one(cfg.ID, cfg.ID // 4)
    return jnp.asarray(np.concatenate(tabs, axis=1), F32)


def _inproj_kernel(src_ref, x_ref, g_ref, sc_ref, sh_ref, ws_ref, rope_ref, wt_hbm, o_ref, h_ref, wbuf, acc_ref, sem,
                   *, patterns, half_a, half_i, n_main):
    i = pl.program_id(0)
    j = pl.program_id(1)
    ni = pl.num_programs(0)
    tn = wbuf.shape[1]
    seq = i * n_main + j

    def tile_copy(jj, slot):
        return pltpu.make_async_copy(wt_hbm.at[pl.ds(pl.multiple_of(src_ref[jj], BF16_ROW_TILE), tn)],
                                     wbuf.at[slot], sem.at[slot])

    @pl.when((i == 0) & (j == 0))
    def _():
        tile_copy(0, 0).start()

    @pl.when(j == 0)
    def _():
        x = x_ref[...]
        ms = jnp.mean(x * x, axis=-1, keepdims=True)
        y = x * lax.rsqrt(ms + NORM_EPS) * g_ref[...]
        h_ref[...] = (y * (1.0 + sc_ref[...]) + sh_ref[...]).astype(BF16)

    dn = (((1,), (1,)), ((), ()))

    @pl.when(j < n_main)
    def _():
        slot = seq % 2
        tile_copy(j, slot).wait()

        @pl.when(j + 1 < n_main)
        def _():
            tile_copy(j + 1, 1 - slot).start()

        @pl.when((j + 1 == n_main) & (i + 1 < ni))
        def _():
            tile_copy(0, 1 - slot).start()

        acc_ref[...] = lax.dot_general(h_ref[...], wbuf[slot], dn, preferred_element_type=F32)

    @pl.when(j >= n_main)
    def _():
        acc_ref[...] = lax.dot_general(h_ref[...], ws_ref[...], dn, preferred_element_type=F32)

    acc = acc_ref[...]
    L = V7X_LANES

    def rot(a, t):
        base, half = (0, half_a) if t == 1 else (3 * L, half_i)
        c = rope_ref[:, base:base + L]
        s1 = rope_ref[:, base + L:base + 2 * L]
        s2 = rope_ref[:, base + 2 * L:base + 3 * L]
        return a * c + pltpu.roll(a, half, 1) * s1 + pltpu.roll(a, L - half, 1) * s2

    for pat, ranges in patterns:
        cond = None
        for lo, hi in ranges:
            c = (j >= lo) & (j < hi)
            cond = c if cond is None else (cond | c)

        @pl.when(cond)
        def _(pat=pat):
            if all(t == 0 for t in pat):
                o_ref[...] = acc
            elif all(t == 3 for t in pat):
                o_ref[...] = _sigmoid(acc)
            else:
                for ci, t in enumerate(pat):
                    a = acc[:, ci * L:(ci + 1) * L]
                    o_ref[:, ci * L:(ci + 1) * L] = a if t == 0 else rot(a, t)


def _tile_patterns(lay: ProjLayout):
    per = lay.tn // V7X_LANES
    ntiles = lay.total // lay.tn
    pats = {}
    for j in range(ntiles):
        pats.setdefault(tuple(lay.chunk_types[j * per:(j + 1) * per]), []).append(j)
    out = []
    for pat, js in pats.items():
        ranges, start, prev = [], js[0], js[0]
        for j in js[1:]:
            if j != prev + 1:
                ranges.append((start, prev + 1))
                start = j
            prev = j
        ranges.append((start, prev + 1))
        out.append((pat, tuple(ranges)))
    return tuple(out)


def in_proj(x, g, sc, sh, w_views, rope, cfg: Cfg, lay: ProjLayout, tm, rows_per_group):
    wt, w_small, tile_src = w_views
    N, D = x.shape
    G, R, _ = sc.shape
    tn = lay.tn
    tiles_per_group = rows_per_group // tm
    n_main = len(tile_src)
    assert N == G * rows_per_group and rows_per_group % tm == 0 and R in (1, tm)
    assert rope.shape[0] == rows_per_group and n_main * tn + w_small.shape[0] == lay.total
    kern = functools.partial(_inproj_kernel, patterns=_tile_patterns(lay), half_a=cfg.HD // 8, half_i=cfg.ID // 8,
                             n_main=n_main)
    mod_spec = pl.BlockSpec((None, R, D), lambda i, j, s: (i // tiles_per_group, 0, 0))
    return pl.pallas_call(
        kern,
        grid_spec=pltpu.PrefetchScalarGridSpec(
            num_scalar_prefetch=1,
            grid=(N // tm, lay.total // tn),
            in_specs=[pl.BlockSpec((tm, D), lambda i, j, s: (i, 0)),
                      pl.BlockSpec((1, D), lambda i, j, s: (0, 0)),
                      mod_spec, mod_spec,
                      pl.BlockSpec((tn, D), lambda i, j, s: (jnp.maximum(j - n_main, 0), 0)),
                      pl.BlockSpec((tm, 6 * V7X_LANES), lambda i, j, s: (i % tiles_per_group, 0)),
                      pl.BlockSpec(memory_space=pl.ANY)],
            out_specs=pl.BlockSpec((tm, tn), lambda i, j, s: (i, j)),
            scratch_shapes=[pltpu.VMEM((tm, D), BF16), pltpu.VMEM((2, tn, D), BF16), pltpu.VMEM((tm, tn), F32),
                            pltpu.SemaphoreType.DMA((2,))]),
        out_shape=jax.ShapeDtypeStruct((N, lay.total), F32),
        compiler_params=_cparams(("arbitrary", "arbitrary")),
        name="in_proj",
    )(jnp.asarray(tile_src), x, g.reshape(1, D), sc, sh, w_small, rope, wt)


def _log_sigmoid(z):
    return jnp.minimum(z, 0.0) - jnp.log1p(jnp.exp(-jnp.abs(z)))


def _gla_prompt_kernel(q_ref, k_ref, v_ref, r_ref, a_ref, wa_ref, ba_ref, gn_ref, o_ref, s_ref, st_ref,
                       *, C, nchunk, rank, scale):
    t = pl.program_id(2)

    @pl.when(t == 0)
    def _():
        st_ref[...] = jnp.zeros_like(st_ref)

    row = lax.broadcasted_iota(I32, (C, C), 0)
    col = lax.broadcasted_iota(I32, (C, C), 1)
    causal = row >= col
    tri = jnp.where(causal, 1.0, 0.0).astype(BF16)
    wa = wa_ref[...]
    ba = ba_ref[...]
    gn = gn_ref[...]
    for c in range(nchunk):
        sl = slice(c * C, (c + 1) * C)
        q = q_ref[0, sl, :]
        k = k_ref[0, sl, :]
        v = v_ref[0, sl, :]
        a = a_ref[0, sl, :][:, :rank]
        la = _log_sigmoid(_bdot(a, wa) + ba) * (1.0 / GLA_TAU)
        hi, mid, lo = _split3(la)
        bc = (jnp.dot(tri, hi, preferred_element_type=F32) + jnp.dot(tri, mid, preferred_element_type=F32)
              + jnp.dot(tri, lo, preferred_element_type=F32))
        bend = bc[C - 1:C, :]
        qd = q * scale * jnp.exp(bc)
        kd = k * jnp.exp(-bc)
        ke = k * jnp.exp(bend - bc)
        att = jnp.where(causal, _dot_nt(qd, kd), 0.0)
        st = st_ref[...]
        o = _bdot(att, v) + _dot_nt(qd, st)
        st_ref[...] = st * jnp.exp(bend) + _dot_tn(v, ke)
        ms = jnp.mean(o * o, axis=-1, keepdims=True)
        on = o * lax.rsqrt(ms + NORM_EPS) * gn
        o_ref[0, sl, :] = on * _silu(r_ref[0, sl, :])

    @pl.when(t == pl.num_programs(2) - 1)
    def _():
        s_ref[0, 0] = st_ref[...].T


def gla_prompt(proj, w_gla_a, b_gla_a, g_gla_norm, cfg: Cfg, lay: ProjLayout):
    B, T, _ = proj.shape
    C = min(GLA_CHUNK, T)
    assert T % C == 0
    tc = C
    while tc * 2 <= 512 and T % (tc * 2) == 0:
        tc *= 2
    DK, DV, GH = cfg.DK, cfg.DV, cfg.GH
    oq, ok, ov, orr, oa = (lay.offs[n] for n in ("gq", "gk", "gv", "gr", "ga"))
    assert oq % DK == 0 and ok % DK == 0 and ov % DV == 0 and orr % DV == 0 and oa % V7X_LANES == 0
    kern = functools.partial(_gla_prompt_kernel, C=C, nchunk=tc // C, rank=cfg.RANK, scale=DK ** -0.5)
    return pl.pallas_call(
        kern,
        grid=(B, GH, T // tc),
        in_specs=[pl.BlockSpec((1, tc, DK), lambda b, h, t: (b, t, oq // DK + h)),
                  pl.BlockSpec((1, tc, DK), lambda b, h, t: (b, t, ok // DK + h)),
                  pl.BlockSpec((1, tc, DV), lambda b, h, t: (b, t, ov // DV + h)),
                  pl.BlockSpec((1, tc, DV), lambda b, h, t: (b, t, orr // DV + h)),
                  pl.BlockSpec((1, tc, V7X_LANES), lambda b, h, t: (b, t, oa // V7X_LANES)),
                  pl.BlockSpec((cfg.RANK, DK), lambda b, h, t: (0, h)),
                  pl.BlockSpec((1, DK), lambda b, h, t: (0, h)),
                  pl.BlockSpec((1, DV), lambda b, h, t: (0, 0))],
        out_specs=[pl.BlockSpec((1, tc, DV), lambda b, h, t: (b, t, h)),
                   pl.BlockSpec((1, 1, DK, DV), lambda b, h, t: (b, h, 0, 0))],
        out_shape=[jax.ShapeDtypeStruct((B, T, GH * DV), F32),
                   jax.ShapeDtypeStruct((B, GH, DK, DV), F32)],
        scratch_shapes=[pltpu.VMEM((DV, DK), F32)],
        compiler_params=_cparams(("arbitrary", "arbitrary", "arbitrary")),
        name="gla_prompt",
    )(proj, proj, proj, proj, proj, w_gla_a, b_gla_a.reshape(1, -1), g_gla_norm.reshape(1, DV))


def _gla_sample_kernel(q_ref, k_ref, v_ref, r_ref, a_ref, wa_ref, ba_ref, gn_ref, s0_ref, o_ref, s_ref,
                       *, GH, DK, DV, rank, scale):
    a = jnp.broadcast_to(a_ref[0][:, :rank], (V7X_SUBLANES, rank))
    ones = jnp.ones((V7X_SUBLANES, V7X_LANES), BF16)
    first = lax.broadcasted_iota(I32, (V7X_SUBLANES, 1), 0) == 0
    for h in range(GH):
        q = q_ref[0][:, h * DK:(h + 1) * DK]
        k = k_ref[0][:, h * DK:(h + 1) * DK]
        v = v_ref[0][:, h * DV:(h + 1) * DV]
        la = _log_sigmoid(_bdot(a, wa_ref[:, h * DK:(h + 1) * DK]) + ba_ref[:, h * DK:(h + 1) * DK])
        la = la[0:1, :] * (1.0 / GLA_TAU)
        dec = jnp.exp(la)
        qd = q * scale * dec
        kd = k * jnp.exp(-la)
        att = jnp.sum(qd * kd, axis=-1, keepdims=True)
        s0 = s0_ref[0, h]
        qd8 = jnp.broadcast_to(qd, (V7X_SUBLANES, DK))
        qh, ql = _split2(qd8)
        sh, sl = _split2(s0)
        qs = (jnp.dot(qh, sh, preferred_element_type=F32) + jnp.dot(qh, sl, preferred_element_type=F32)
              + jnp.dot(ql, sh, preferred_element_type=F32))[0:1, :]
        o = att * v + qs
        def col(x, width):
            x8 = jnp.where(first, jnp.broadcast_to(x, (V7X_SUBLANES, x.shape[1])), 0.0)
            parts = _split3(x8)
            rhs = ones if width == V7X_LANES else jnp.ones((V7X_SUBLANES, width), BF16)
            acc = None
            for p in parts:
                d = lax.dot_general(p, rhs, (((0,), (0,)), ((), ())), preferred_element_type=F32)
                acc = d if acc is None else acc + d
            return acc

        dec_col = col(dec, V7X_LANES)[:, 0:1]
        k8 = jnp.where(first, jnp.broadcast_to(k, (V7X_SUBLANES, DK)), 0.0)
        v8 = jnp.where(first, jnp.broadcast_to(v, (V7X_SUBLANES, DV)), 0.0)
        kh, kl = _split2(k8)
        vh, vl = _split2(v8)
        dn = (((0,), (0,)), ((), ()))
        outer = (lax.dot_general(kh, vh, dn, preferred_element_type=F32)
                 + lax.dot_general(kh, vl, dn, preferred_element_type=F32)
                 + lax.dot_general(kl, vh, dn, preferred_element_type=F32))
        s_ref[0, h] = s0 * dec_col + outer
        ms = jnp.mean(o * o, axis=-1, keepdims=True)
        on = o * lax.rsqrt(ms + NORM_EPS) * gn_ref[...]
        o_ref[0, :, h * DV:(h + 1) * DV] = on * _silu(r_ref[0][:, h * DV:(h + 1) * DV])


def gla_sample(proj, state, w_gla_a, b_gla_a, g_gla_norm, cfg: Cfg, lay: ProjLayout):
    DB = proj.shape[0]
    DK, DV, GH = cfg.DK, cfg.DV, cfg.GH
    qk, vv = GH * DK, GH * DV
    oq, ok, ov, orr, oa = (lay.offs[n] for n in ("gq", "gk", "gv", "gr", "ga"))
    assert oq % qk == 0 and ok % qk == 0 and ov % vv == 0 and orr % vv == 0
    kern = functools.partial(_gla_sample_kernel, GH=GH, DK=DK, DV=DV, rank=cfg.RANK, scale=DK ** -0.5)
    return pl.pallas_call(
        kern,
        grid=(DB,),
        in_specs=[pl.BlockSpec((1, 1, qk), lambda b: (b, 0, oq // qk)),
                  pl.BlockSpec((1, 1, qk), lambda b: (b, 0, ok // qk)),
                  pl.BlockSpec((1, 1, vv), lambda b: (b, 0, ov // vv)),
                  pl.BlockSpec((1, 1, vv), lambda b: (b, 0, orr // vv)),
                  pl.BlockSpec((1, 1, V7X_LANES), lambda b: (b, 0, oa // V7X_LANES)),
                  pl.BlockSpec((cfg.RANK, qk), lambda b: (0, 0)),
                  pl.BlockSpec((1, qk), lambda b: (0, 0)),
                  pl.BlockSpec((1, DV), lambda b: (0, 0)),
                  pl.BlockSpec((1, GH, DK, DV), lambda b: (b, 0, 0, 0))],
        out_specs=[pl.BlockSpec((1, 1, vv), lambda b: (b, 0, 0)),
                   pl.BlockSpec((1, GH, DK, DV), lambda b: (b, 0, 0, 0))],
        out_shape=[jax.ShapeDtypeStruct((DB, 1, vv), F32),
                   jax.ShapeDtypeStruct((DB, GH, DK, DV), F32)],
        compiler_params=_cparams(("arbitrary",)),
        name="gla_sample",
    )(proj, proj, proj, proj, proj, w_gla_a, b_gla_a.reshape(1, -1), g_gla_norm.reshape(1, DV), state)


def _offset_word_to_f32(t):
    key = t ^ INT_MIN
    return pltpu.bitcast(key ^ ((key >> 31) & INT_MAX), F32)


def _dsa_prompt_kernel(iq_ref, iw_ref, aq_ref, kidx_ref, kbf_ref, vbf_ref, o_ref,
                       sc_ref, lg_ref, q2_ref, wb_ref, *, IH, ID, NH, G, HD, topk, wscale, sm_scale, KEY_CHUNK):
    qb = pl.program_id(1)
    QB = Q_BLOCK
    L = V7X_LANES
    CH = KEY_CHUNK // L
    nch = qb // CH + 1
    lane = lax.broadcasted_iota(I32, (QB, L), 1)
    rowq = lax.broadcasted_iota(I32, (QB, 1), 0) + qb * QB
    first = lane < ID

    for h in range(IH):
        pair = iq_ref[0][:, (h // 2) * L:(h // 2 + 1) * L]
        qq = jnp.where((lane // ID) == (h % 2), pair, pltpu.roll(pair, ID, 1))
        hi, lo = _split2(qq)
        q2_ref[h * QB:(h + 1) * QB, 0:L] = jnp.where(first, hi, lo)
        q2_ref[h * QB:(h + 1) * QB, L:2 * L] = jnp.where(first, hi, jnp.zeros_like(hi))
        wb_ref[h * QB:(h + 1) * QB, :] = jnp.broadcast_to(iw_ref[0][:, h:h + 1] * wscale, (QB, 2 * L))

    lane2 = lax.broadcasted_iota(I32, (QB, 2 * L), 1)
    first_k = lax.broadcasted_iota(I32, (2 * L, L), 1) < ID

    def score_body(c, carry):
        for half in range(CH // 2):
            k0 = c * KEY_CHUNK + half * 2 * L
            cols = slice(half * 2 * L, (half + 1) * 2 * L)

            @pl.when(k0 < (qb + 1) * QB)
            def _():
                kblk = kidx_ref[0, pl.ds(pl.multiple_of(k0, 2 * L), 2 * L), :]
                khi, klo = _split2(kblk)
                k2 = jnp.concatenate([khi, jnp.where(first_k, klo, jnp.zeros_like(klo))], axis=1)
                s = lax.dot_general(q2_ref[...], k2, (((1,), (1,)), ((), ())),
                                    preferred_element_type=F32)
                s = jnp.maximum(s, 0.0) * wb_ref[...]
                sc = s[0:QB]
                for h in range(1, IH):
                    sc = sc + s[h * QB:(h + 1) * QB]
                sc_ref[c, :, cols] = jnp.where(k0 + lane2 <= rowq, sc, -jnp.inf)

            @pl.when(k0 >= (qb + 1) * QB)
            def _():
                sc_ref[c, :, cols] = jnp.full((QB, 2 * L), -jnp.inf, F32)
        return carry

    lax.fori_loop(0, nch, score_body, 0)

    def sweep(fn, init, combine):
        def body(c, acc):
            slab = sc_ref[c]
            for j in range(CH):
                acc = combine(acc, fn(slab[:, j * L:(j + 1) * L], c * KEY_CHUNK + j * L + lane))
            return acc
        return lax.fori_loop(0, nch, body, init)

    def count(pred):
        acc = sweep(lambda s, pos: jnp.where(pred(s, pos), 1, 0), jnp.zeros((QB, L), I32), jnp.add)
        return jnp.sum(acc, axis=1, keepdims=True)

    want = jnp.minimum(rowq + 1, topk)

    def bit_body(i, t):
        cand = t | (jnp.int32(1) << (31 - i))
        cf = _offset_word_to_f32(cand)
        cnt = count(lambda s, pos: s >= cf)
        return jnp.where(cnt >= want, cand, t)

    thr0 = _offset_word_to_f32(lax.fori_loop(0, 32, bit_body, jnp.zeros((QB, 1), I32)))
    n_ge = count(lambda s, pos: s >= thr0)
    thr = jnp.min(sweep(lambda s, pos: jnp.where(s >= thr0, s, jnp.inf), jnp.full((QB, L), jnp.inf, F32),
                        jnp.minimum), axis=1, keepdims=True)
    cut = jnp.full((QB, 1), INT_MAX, I32)

    def selected(s, pos, thr, cut):
        return (s > thr) | ((s == thr) & (pos <= cut))

    def drop_last(_, tc):
        thr, cut = tc
        over = count(lambda s, pos: selected(s, pos, thr, cut)) > want
        low = jnp.min(sweep(lambda s, pos: jnp.where(selected(s, pos, thr, cut), s, jnp.inf),
                            jnp.full((QB, L), jnp.inf, F32), jnp.minimum), axis=1, keepdims=True)
        last = jnp.max(sweep(lambda s, pos: jnp.where(selected(s, pos, thr, cut) & (s == low), pos, -1),
                             jnp.full((QB, L), -1, I32), jnp.maximum), axis=1, keepdims=True)
        return jnp.where(over, low, thr), jnp.where(over, last - 1, cut)

    thr, cut = lax.fori_loop(0, jnp.max(n_ge - want), drop_last, (thr, cut))

    def mask_body(c, carry):
        slab = sc_ref[c]
        pos = c * KEY_CHUNK + lax.broadcasted_iota(I32, slab.shape, 1)
        sc_ref[c] = jnp.where(selected(slab, pos, thr, cut), 0.0, -jnp.inf)
        return carry

    lax.fori_loop(0, nch, mask_body, 0)

    c2 = sm_scale * math.log2(math.e)
    for g in range(NH // G):
        qg = jnp.concatenate([aq_ref[0][:, (g * G + j) * HD:(g * G + j + 1) * HD] for j in range(G)],
                             axis=0).astype(BF16)

        def logit_body(c, mx):
            rows = pl.ds(pl.multiple_of(c * KEY_CHUNK, KEY_CHUNK), KEY_CHUNK)
            s = lax.dot_general(qg, kbf_ref[0, rows, g * HD:(g + 1) * HD], (((1,), (1,)), ((), ())),
                                preferred_element_type=F32) * c2
            mask = sc_ref[c]
            for j in range(G):
                sj = s[j * QB:(j + 1) * QB] + mask
                lg_ref[c, j * QB:(j + 1) * QB, :] = sj
                for b in range(CH):
                    mx[j] = jnp.maximum(mx[j], sj[:, b * L:(b + 1) * L])
            return mx

        mx = lax.fori_loop(0, nch, logit_body, [jnp.full((QB, L), NEG_BIG, F32) for _ in range(G)])
        m = jnp.concatenate([jnp.broadcast_to(jnp.max(x, axis=1, keepdims=True), (QB, L)) for x in mx],
                            axis=0)

        def value_body(c, carry):
            ls, acc = carry
            rows = pl.ds(pl.multiple_of(c * KEY_CHUNK, KEY_CHUNK), KEY_CHUNK)
            ps = [jnp.exp2(lg_ref[c, :, b * L:(b + 1) * L] - m) for b in range(CH)]
            for p in ps:
                ls = ls + p
            p = jnp.concatenate(ps, axis=1).astype(BF16)
            acc = acc + jnp.dot(p, vbf_ref[0, rows, g * HD:(g + 1) * HD], preferred_element_type=F32)
            return ls, acc

        ls, acc = lax.fori_loop(0, nch, value_body, (jnp.zeros((G * QB, L), F32), jnp.zeros((G * QB, HD), F32)))
        out = acc / jnp.sum(ls, axis=1, keepdims=True)
        for j in range(G):
            o_ref[0, :, (g * G + j) * HD:(g * G + j + 1) * HD] = out[j * QB:(j + 1) * QB]


def dsa_prompt(proj, kbf, vbf, cfg: Cfg, lay: ProjLayout):
    B, T, _ = proj.shape
    QB = Q_BLOCK
    chunk = min(KEY_CHUNK, T)
    assert T % chunk == 0 and chunk % (2 * V7X_LANES) == 0 and cfg.ID * 2 == V7X_LANES and cfg.IH % 2 == 0
    topk = min(TOPK_MAX, T // 4)
    iqw, aqw, kvw = cfg.IH * cfg.ID, cfg.NH * cfg.HD, cfg.NKV * cfg.HD
    oiq, oaq, oik, oiw = (lay.offs[n] for n in ("iq", "aq", "ik", "iw"))
    assert oiq % iqw == 0 and oaq % aqw == 0
    G = cfg.NH // cfg.NKV
    kern = functools.partial(_dsa_prompt_kernel, IH=cfg.IH, ID=cfg.ID, NH=cfg.NH, G=G, HD=cfg.HD, topk=topk,
                             wscale=(cfg.IH ** -0.5) * (cfg.ID ** -0.5), sm_scale=cfg.HD ** -0.5, KEY_CHUNK=chunk)
    return pl.pallas_call(
        kern,
        grid=(B, T // QB),
        in_specs=[pl.BlockSpec((1, QB, iqw), lambda b, q: (b, q, oiq // iqw)),
                  pl.BlockSpec((1, QB, V7X_LANES), lambda b, q: (b, q, oiw // V7X_LANES)),
                  pl.BlockSpec((1, QB, aqw), lambda b, q: (b, q, oaq // aqw)),
                  pl.BlockSpec((1, T, V7X_LANES), lambda b, q: (b, 0, oik // V7X_LANES)),
                  pl.BlockSpec((1, T, kvw), lambda b, q: (b, 0, 0)),
                  pl.BlockSpec((1, T, kvw), lambda b, q: (b, 0, 0))],
        out_specs=pl.BlockSpec((1, QB, aqw), lambda b, q: (b, q, 0)),
        out_shape=jax.ShapeDtypeStruct((B, T, aqw), F32),
        scratch_shapes=[pltpu.VMEM((T // chunk, QB, chunk), F32),
                        pltpu.VMEM((T // chunk, G * QB, chunk), F32),
                        pltpu.VMEM((cfg.IH * QB, 2 * V7X_LANES), BF16),
                        pltpu.VMEM((cfg.IH * QB, 2 * V7X_LANES), F32)],
        compiler_params=_cparams(("arbitrary", "arbitrary")),
        name="dsa_prompt",
    )(proj, proj, proj, proj, kbf, vbf)


def _dsa_sample_select_kernel(pt_ref, q_ref, w_ref, kn_ref, kidx_hbm, idx_ref, kbuf, sc_ref, rank_ref, sel_ref, sem,
                              *, NP, NPP, PG, topk, wscale):
    b = pl.program_id(0)
    nb = pl.num_programs(0)
    slot = b % 2
    L = V7X_LANES

    def page_copy(sl, p, page):
        return pltpu.make_async_copy(kidx_hbm.at[0, page], kbuf.at[sl, p], sem.at[sl])

    def issue(bb, sl):
        def body(p, c):
            page_copy(sl, p, pt_ref[bb, p]).start()
            return c
        lax.fori_loop(0, NP, body, 0, unroll=math.gcd(NP, 2 * DMA_LOOP_UNROLL))

    @pl.when(b == 0)
    def _():
        issue(0, 0)

    @pl.when(b + 1 < nb)
    def _():
        issue(b + 1, 1 - slot)

    def wait_body(p, c):
        page_copy(slot, p, 0).wait()
        return c

    lax.fori_loop(0, NP, wait_body, 0, unroll=math.gcd(NP, 2 * DMA_LOOP_UNROLL))

    q = q_ref[0]
    w = w_ref[0][:, 0:1] * wscale
    qh, ql = _split2(q)
    q3 = jnp.concatenate([qh, ql, qh], axis=1)
    dn = (((1,), (1,)), ((), ()))

    def group_body(gi, c):
        kk = kbuf[slot, pl.ds(gi * PG, PG)]
        kk = jnp.concatenate([kk[j] for j in range(PG)], axis=1)
        kh, kl = _split2(kk)
        s = jnp.dot(q3, jnp.concatenate([kh, kh, kl], axis=0), preferred_element_type=F32)
        sc = jnp.sum(jnp.maximum(s, 0.0) * w, axis=0, keepdims=True)
        for j in range(PG):
            sc_ref[pl.ds(gi * PG + j, 1), :] = sc[:, j * L:(j + 1) * L]
        return c

    lax.fori_loop(0, NP // PG, group_body, 0)
    sn = jnp.sum(q * kn_ref[0], axis=1, keepdims=True)
    scn = jnp.sum(jnp.maximum(sn, 0.0) * w, axis=0, keepdims=True)
    tail = (NPP - NP, L)
    first = (lax.broadcasted_iota(I32, tail, 0) == 0) & (lax.broadcasted_iota(I32, tail, 1) == 0)
    sc_ref[NP:NPP, :] = jnp.where(first, scn, -jnp.inf)

    sc = sc_ref[...]
    pos = lax.broadcasted_iota(I32, sc.shape, 0) * L + lax.broadcasted_iota(I32, sc.shape, 1)

    def bit_body(i, t):
        cand = t | (jnp.int32(1) << (31 - i))
        hit = jnp.where(sc >= _offset_word_to_f32(cand), 1, 0)
        cnt = jnp.sum(jnp.sum(hit, axis=0, keepdims=True), axis=1, keepdims=True)
        return jnp.where(cnt >= topk, cand, t)

    thr0 = _offset_word_to_f32(lax.fori_loop(0, 32, bit_body, jnp.zeros((1, L), I32)))
    n_ge = jnp.sum(jnp.where(sc >= thr0, 1, 0))
    thr = jnp.min(jnp.where(sc >= thr0, sc, jnp.inf))
    cut = jnp.int32(INT_MAX)

    def selected(thr, cut):
        return (sc > thr) | ((sc == thr) & (pos <= cut))

    def drop_last(_, tc):
        thr, cut = tc
        sel = selected(thr, cut)
        low = jnp.min(jnp.where(sel, sc, jnp.inf))
        last = jnp.max(jnp.where(sel & (sc == low), pos, -1))
        return low, last - 1

    thr, cut = lax.fori_loop(0, n_ge - topk, drop_last, (thr, cut))
    sel = selected(thr, cut)

    self_ = jnp.where(sel, 1.0, 0.0)
    li = lax.broadcasted_iota(I32, (L, L), 0)
    lj = lax.broadcasted_iota(I32, (L, L), 1)
    before_lane = jnp.where(li < lj, 1.0, 0.0).astype(BF16)
    ri = lax.broadcasted_iota(I32, (NPP, NPP), 0)
    rj = lax.broadcasted_iota(I32, (NPP, NPP), 1)
    before_row = jnp.where(rj < ri, 1.0, 0.0).astype(BF16)
    rowtot = jnp.broadcast_to(jnp.sum(self_, axis=1, keepdims=True), (NPP, L))
    rank_ref[...] = (jnp.dot(self_.astype(BF16), before_lane, preferred_element_type=F32)
                     + jnp.dot(before_row, rowtot.astype(BF16), preferred_element_type=F32))
    sel_ref[...] = self_

    slot_id = lax.broadcasted_iota(I32, (topk, 1), 0).astype(F32)
    lane1 = lax.broadcasted_iota(I32, (1, L), 1)

    def place(r, acc):
        hit = (rank_ref[pl.ds(r, 1), :] == slot_id) & (sel_ref[pl.ds(r, 1), :] > 0.0)
        return acc + jnp.where(hit, (r * L + lane1).astype(F32), 0.0)

    placed = lax.fori_loop(0, NPP, place, jnp.zeros((topk, L), F32)).astype(I32)
    ones = jnp.ones((V7X_SUBLANES, L), BF16)
    lane_bits = L.bit_length() - 1
    hi = (placed >> lane_bits).astype(F32).astype(BF16)
    lo = (placed & (L - 1)).astype(F32).astype(BF16)
    idx = (lax.dot_general(ones, hi, dn, preferred_element_type=F32) * float(L)
           + lax.dot_general(ones, lo, dn, preferred_element_type=F32))
    idx_ref[0] = idx.astype(I32)


def _dsa_sample_attend_kernel(idx_s, pt_s, q_ref, idx_ref, kn_ref, vn_ref, ck_hbm, cv_hbm, o_ref, kbuf, vbuf, sem,
                              *, NKV, G, topk, past, page_shift, sm_scale):
    b = pl.program_id(0)
    nb = pl.num_programs(0)
    slot = b % 2
    page_mask = (1 << page_shift) - 1

    def row_copies(sl, j, page, r):
        return (pltpu.make_async_copy(ck_hbm.at[0, page, r], kbuf.at[sl, j], sem.at[0, sl]),
                pltpu.make_async_copy(cv_hbm.at[0, page, r], vbuf.at[sl, j], sem.at[1, sl]))

    def issue(bb, sl):
        def body(j, c):
            i = jnp.minimum(idx_s[bb, j], past - 1)
            for cp in row_copies(sl, j, pt_s[bb, i >> page_shift], i & page_mask):
                cp.start()
            return c
        lax.fori_loop(0, topk, body, 0, unroll=2 * DMA_LOOP_UNROLL)

    @pl.when(b == 0)
    def _():
        issue(0, 0)

    @pl.when(b + 1 < nb)
    def _():
        issue(b + 1, 1 - slot)

    def wait_body(j, c):
        for cp in row_copies(slot, j, 0, 0):
            cp.wait()
        return c

    lax.fori_loop(0, topk, wait_body, 0, unroll=2 * DMA_LOOP_UNROLL)

    idx = idx_ref[0][0:1, :]
    cached = idx < past
    new_sel = jnp.max(jnp.where(idx == past, 1, 0), axis=1, keepdims=True) > 0
    for g in range(NKV):
        qg = q_ref[0][g * G:(g + 1) * G, :]
        kg = kbuf[slot, :, g, :]
        vg = vbuf[slot, :, g, :]
        s = jnp.where(cached, _dot_nt(qg, kg) * sm_scale, -jnp.inf)
        sn = jnp.sum(qg * kn_ref[0][g:g + 1, :], axis=1, keepdims=True) * sm_scale
        sn = jnp.where(new_sel, sn, -jnp.inf)
        m = jnp.maximum(jnp.max(s, axis=1, keepdims=True), sn)
        p = jnp.exp(s - m)
        pn = jnp.exp(sn - m)
        l = jnp.sum(p, axis=1, keepdims=True) + pn
        o_ref[0, g * G:(g + 1) * G, :] = (_bdot(p, vg) + pn * vn_ref[0][g:g + 1, :]) / l


def dsa_sample(q_att, q_idx, w_idx, kn_idx, k_new, v_new, cache_k, cache_v, cache_kidx, page_table, cfg: Cfg):
    DB, NP, PAGE = cfg.DB, cfg.NPAGES, cfg.PAGE
    assert PAGE == V7X_LANES and cache_k.shape[0] == 1
    total = cfg.PAST + 1
    topk = min(TOPK_MAX, total // 4)
    assert topk <= total and topk % V7X_LANES == 0
    NPP = -(-(NP + 1) // V7X_SUBLANES) * V7X_SUBLANES
    PG = math.gcd(NP, 8)
    w_b = jnp.broadcast_to(w_idx[:, :, None], (DB, cfg.IH, V7X_LANES))
    sel_kern = functools.partial(_dsa_sample_select_kernel, NP=NP, NPP=NPP, PG=PG, topk=topk,
                                 wscale=(cfg.IH ** -0.5) * (cfg.ID ** -0.5))
    idx = pl.pallas_call(
        sel_kern,
        grid_spec=pltpu.PrefetchScalarGridSpec(
            num_scalar_prefetch=1,
            grid=(DB,),
            in_specs=[pl.BlockSpec((1, cfg.IH, cfg.ID), lambda b, pt: (b, 0, 0)),
                      pl.BlockSpec((1, cfg.IH, V7X_LANES), lambda b, pt: (b, 0, 0)),
                      pl.BlockSpec((1, 1, cfg.ID), lambda b, pt: (b, 0, 0)),
                      pl.BlockSpec(memory_space=pl.ANY)],
            out_specs=pl.BlockSpec((1, V7X_SUBLANES, topk), lambda b, pt: (b, 0, 0)),
            scratch_shapes=[pltpu.VMEM((2, NP, cfg.ID, PAGE), F32),
                            pltpu.VMEM((NPP, V7X_LANES), F32),
                            pltpu.VMEM((NPP, V7X_LANES), F32),
                            pltpu.VMEM((NPP, V7X_LANES), F32),
                            pltpu.SemaphoreType.DMA((2,))]),
        out_shape=jax.ShapeDtypeStruct((DB, V7X_SUBLANES, topk), I32),
        compiler_params=_cparams(("arbitrary",)),
        name="dsa_sample_select",
    )(page_table, q_idx, w_b, kn_idx.reshape(DB, 1, cfg.ID), jnp.swapaxes(cache_kidx, 2, 3))

    G = cfg.NH // cfg.NKV
    att_kern = functools.partial(_dsa_sample_attend_kernel, NKV=cfg.NKV, G=G, topk=topk, past=cfg.PAST,
                                 page_shift=PAGE.bit_length() - 1, sm_scale=cfg.HD ** -0.5)
    return pl.pallas_call(
        att_kern,
        grid_spec=pltpu.PrefetchScalarGridSpec(
            num_scalar_prefetch=2,
            grid=(DB,),
            in_specs=[pl.BlockSpec((1, cfg.NH, cfg.HD), lambda b, ix, pt: (b, 0, 0)),
                      pl.BlockSpec((1, V7X_SUBLANES, topk), lambda b, ix, pt: (b, 0, 0)),
                      pl.BlockSpec((1, cfg.NKV, cfg.HD), lambda b, ix, pt: (b, 0, 0)),
                      pl.BlockSpec((1, cfg.NKV, cfg.HD), lambda b, ix, pt: (b, 0, 0)),
                      pl.BlockSpec(memory_space=pl.ANY),
                      pl.BlockSpec(memory_space=pl.ANY)],
            out_specs=pl.BlockSpec((1, cfg.NH, cfg.HD), lambda b, ix, pt: (b, 0, 0)),
            scratch_shapes=[pltpu.VMEM((2, topk, cfg.NKV, cfg.HD), F32),
                            pltpu.VMEM((2, topk, cfg.NKV, cfg.HD), F32),
                            pltpu.SemaphoreType.DMA((2, 2))]),
        out_shape=jax.ShapeDtypeStruct((DB, cfg.NH, cfg.HD), F32),
        compiler_params=_cparams(("arbitrary",)),
        name="dsa_sample_attend",
    )(idx[:, 0, :], page_table, q_att, idx, k_new.reshape(DB, cfg.NKV, cfg.HD), v_new.reshape(DB, cfg.NKV, cfg.HD),
      cache_k, cache_v)


def _merge_kernel(og_ref, oa_ref, wg_ref, wa_ref, za_ref, zb_ref, o_ref, ogb_ref, oab_ref):
    @pl.when(pl.program_id(1) == 0)
    def _():
        ogb_ref[...] = og_ref[...].astype(BF16)
        oab_ref[...] = oa_ref[...].astype(BF16)

    a = jnp.dot(ogb_ref[...], wg_ref[...].astype(BF16), preferred_element_type=F32)
    b = jnp.dot(oab_ref[...], wa_ref[...].astype(BF16), preferred_element_type=F32)
    o_ref[...] = za_ref[...] * a + zb_ref[...] * b


def merge_branches(og, oa, w_gla_o, w_attn_o, proj, cfg: Cfg, lay: ProjLayout, tm):
    N = og.shape[0]
    D, tn = cfg.D, lay.tn
    oza, ozb = lay.offs["za"] // tn, lay.offs["zb"] // tn
    return pl.pallas_call(
        _merge_kernel,
        grid=(N // tm, D // tn),
        in_specs=[pl.BlockSpec((tm, og.shape[1]), lambda i, j: (i, 0)),
                  pl.BlockSpec((tm, oa.shape[1]), lambda i, j: (i, 0)),
                  pl.BlockSpec((og.shape[1], tn), lambda i, j: (0, j)),
                  pl.BlockSpec((oa.shape[1], tn), lambda i, j: (0, j)),
                  pl.BlockSpec((tm, tn), lambda i, j: (i, oza + j)),
                  pl.BlockSpec((tm, tn), lambda i, j: (i, ozb + j))],
        out_specs=pl.BlockSpec((tm, tn), lambda i, j: (i, j)),
        out_shape=jax.ShapeDtypeStruct((N, D), F32),
        scratch_shapes=[pltpu.VMEM((tm, og.shape[1]), BF16), pltpu.VMEM((tm, oa.shape[1]), BF16)],
        compiler_params=_cparams(("arbitrary", "arbitrary")),
        name="merge_branches",
    )(og, oa, w_gla_o, w_attn_o, proj, proj)


def _outproj_kernel(m_ref, w_ref, x_ref, gt_ref, o_ref, mb_ref):
    @pl.when(pl.program_id(1) == 0)
    def _():
        mb_ref[...] = m_ref[...].astype(BF16)

    o_ref[...] = x_ref[...] + gt_ref[...] * jnp.dot(mb_ref[...], w_ref[...].astype(BF16), preferred_element_type=F32)


def out_proj(merged, w_out, x, gt, tm, rows_per_group, tn):
    N, D = x.shape
    G, R, _ = gt.shape
    tpg = rows_per_group // tm
    return pl.pallas_call(
        _outproj_kernel,
        grid=(N // tm, D // tn),
        in_specs=[pl.BlockSpec((tm, D), lambda i, j: (i, 0)),
                  pl.BlockSpec((D, tn), lambda i, j: (0, j)),
                  pl.BlockSpec((tm, tn), lambda i, j: (i, j)),
                  pl.BlockSpec((None, R, tn), lambda i, j: (i // tpg, 0, j))],
        out_specs=pl.BlockSpec((tm, tn), lambda i, j: (i, j)),
        out_shape=jax.ShapeDtypeStruct((N, D), F32),
        scratch_shapes=[pltpu.VMEM((tm, D), BF16)],
        compiler_params=_cparams(("arbitrary", "arbitrary")),
        name="out_proj",
    )(merged, w_out, x, gt)


def _ffn_pre_kernel(x_ref, g_ref, sc_ref, sh_ref, wrt_ref, br_ref, wg_ref, wu_ref, wd_ref,
                    h2_ref, ysh_ref, eidx_ref, wts_ref,
                    wrh_ref, wrl_ref, wgb_ref, wub_ref, wdb_ref, *, E, NG, first_tile):
    @pl.when(pl.program_id(0) == first_tile)
    def _():
        hi, lo = _split2(wrt_ref[...])
        wrh_ref[...] = hi
        wrl_ref[...] = lo
        wgb_ref[...] = wg_ref[...].astype(BF16)
        wub_ref[...] = wu_ref[...].astype(BF16)
        wdb_ref[...] = wd_ref[...].astype(BF16)

    x = x_ref[...]
    tm = x.shape[0]
    ms = jnp.mean(x * x, axis=-1, keepdims=True)
    h2 = x * lax.rsqrt(ms + NORM_EPS) * g_ref[...] * (1.0 + sc_ref[...]) + sh_ref[...]
    h2_ref[...] = h2
    hh, hl = _split2(h2)

    gate = jnp.dot(hh, wgb_ref[...], preferred_element_type=F32)
    up = jnp.dot(hh, wub_ref[...], preferred_element_type=F32)
    ysh_ref[...] = jnp.dot((_silu(gate) * up).astype(BF16), wdb_ref[...], preferred_element_type=F32)

    dn = (((1,), (1,)), ((), ()))
    logit = (lax.dot_general(wrh_ref[...], hh, dn, preferred_element_type=F32)
             + lax.dot_general(wrh_ref[...], hl, dn, preferred_element_type=F32)
             + lax.dot_general(wrl_ref[...], hh, dn, preferred_element_type=F32))
    s = _sigmoid(logit)
    PG = E // NG
    s3 = s.reshape(NG, PG, tm)
    sel3 = (s + br_ref[:, 0:1]).reshape(NG, PG, tm)
    ninf = -jnp.inf
    ipg = lax.broadcasted_iota(I32, (NG, PG, tm), 1)
    m1 = jnp.max(sel3, axis=1, keepdims=True)
    i1 = jnp.min(jnp.where(sel3 == m1, ipg, PG), axis=1, keepdims=True)
    m2 = jnp.max(jnp.where(ipg == i1, ninf, sel3), axis=1, keepdims=True)
    gs = (m1 + m2).reshape(NG, tm)
    ig = lax.broadcasted_iota(I32, (NG, tm), 0)
    gmask = jnp.zeros((NG, tm), jnp.bool_)
    for _ in range(TOPK_GROUPS):
        m = jnp.max(gs, axis=0, keepdims=True)
        i = jnp.min(jnp.where(gs == m, ig, NG), axis=0, keepdims=True)
        pick = ig == i
        gmask = gmask | pick
        gs = jnp.where(pick, ninf, gs)
    cand = jnp.where(gmask.reshape(NG, 1, tm), sel3, ninf)
    ie = lax.broadcasted_iota(I32, (NG, PG, tm), 0) * PG + ipg
    idxs, ws = [], []
    for _ in range(TOP_K):
        m = jnp.max(jnp.max(cand, axis=0, keepdims=True), axis=1, keepdims=True)
        hit = jnp.where(cand == m, ie, E)
        i = jnp.min(jnp.min(hit, axis=0, keepdims=True), axis=1, keepdims=True)
        pick = ie == i
        wsel = jnp.where(pick, s3, 0.0)
        ws.append(jnp.sum(jnp.sum(wsel, axis=0, keepdims=True), axis=1, keepdims=True).reshape(1, tm))
        idxs.append(i.reshape(1, tm))
        cand = jnp.where(pick, ninf, cand)
    tot = ws[0]
    for w in ws[1:]:
        tot = tot + w
    for k in range(TOP_K):
        eidx_ref[k:k + 1, :] = idxs[k]
        wts_ref[k:k + 1, :] = ws[k] / tot * ROUTED_SCALE


def ffn_pre(x1, g, sc, sh, w_router_t, b_router, w_sh_gate, w_sh_up, w_sh_down, cfg: Cfg, tm):
    N, D = x1.shape
    G, R, _ = sc.shape
    tpg = (N // G) // tm
    E, DE = cfg.E, cfg.DE
    assert E // N_GROUPS == V7X_SUBLANES
    mod_spec = pl.BlockSpec((None, R, D), lambda i: (i // tpg, 0, 0))
    const = lambda shape: pl.BlockSpec(shape, lambda i: (0,) * len(shape))
    return pl.pallas_call(
        functools.partial(_ffn_pre_kernel, E=E, NG=N_GROUPS, first_tile=0),
        grid=(N // tm,),
        in_specs=[pl.BlockSpec((tm, D), lambda i: (i, 0)), const((1, D)), mod_spec, mod_spec,
                  const((E, D)), const((E, V7X_LANES)), const((D, DE)), const((D, DE)), const((DE, D))],
        out_specs=[pl.BlockSpec((tm, D), lambda i: (i, 0)),
                   pl.BlockSpec((tm, D), lambda i: (i, 0)),
                   pl.BlockSpec((TOP_K, tm), lambda i: (0, i)),
                   pl.BlockSpec((TOP_K, tm), lambda i: (0, i))],
        out_shape=[jax.ShapeDtypeStruct((N, D), F32),
                   jax.ShapeDtypeStruct((N, D), F32),
                   jax.ShapeDtypeStruct((TOP_K, N), I32),
                   jax.ShapeDtypeStruct((TOP_K, N), F32)],
        scratch_shapes=[pltpu.VMEM((E, D), BF16), pltpu.VMEM((E, D), BF16),
                        pltpu.VMEM((D, DE), BF16), pltpu.VMEM((D, DE), BF16), pltpu.VMEM((DE, D), BF16)],
        compiler_params=_cparams(("arbitrary",)),
        name="ffn_pre",
    )(x1, g.reshape(1, D), sc, sh, w_router_t, jnp.broadcast_to(b_router[:, None], (E, V7X_LANES)),
      w_sh_gate, w_sh_up, w_sh_down)


def _rank_kernel(eidx_ref, rank_ref, cnt_ref, carry_ref, *, E):
    i = pl.program_id(0)

    @pl.when(i == 0)
    def _():
        carry_ref[...] = jnp.zeros(carry_ref.shape, F32)

    eidx = eidx_ref[...]
    tm = eidx.shape[1]
    eio = lax.broadcasted_iota(I32, (E, 1), 0)
    onehot = jnp.zeros((E, tm), F32)
    for k in range(TOP_K):
        onehot = onehot + jnp.where(eidx[k:k + 1, :] == eio, 1.0, 0.0)
    r = lax.broadcasted_iota(I32, (tm, tm), 0)
    c = lax.broadcasted_iota(I32, (tm, tm), 1)
    tri = jnp.where(r <= c, 1.0, 0.0).astype(BF16)
    cum = jnp.dot(onehot.astype(BF16), tri, preferred_element_type=F32)
    excl = cum - onehot + carry_ref[:, 0:1]
    for k in range(TOP_K):
        hit = eidx[k:k + 1, :] == eio
        rank_ref[k:k + 1, :] = jnp.sum(jnp.where(hit, excl, 0.0), axis=0, keepdims=True).astype(I32)
    carry_ref[...] = carry_ref[...] + cum[:, tm - 1:tm]

    @pl.when(i == pl.num_programs(0) - 1)
    def _():
        cnt_ref[...] = carry_ref[...].astype(I32)


def _plan_kernel(eidx_ref, rank_ref, cnt_ref, dest_ref, be_ref, nr_ref, *, E, bm, NBP):
    L = V7X_LANES
    sh = bm.bit_length() - 1
    cnt = cnt_ref[...]
    padded = ((cnt + (bm - 1)) >> sh) << sh
    r = lax.broadcasted_iota(I32, (E, E), 0)
    c = lax.broadcasted_iota(I32, (E, E), 1)
    low = jnp.where(c < r, 1.0, 0.0).astype(BF16)
    pstart = jnp.zeros((E, L), F32)
    for shift in (0, 8, 16):
        piece = ((padded >> shift) & 255).astype(F32).astype(BF16)
        pstart = pstart + jnp.dot(low, piece, preferred_element_type=F32) * float(1 << shift)
    pstart = pstart.astype(I32)
    pend = pstart + padded
    eio = lax.broadcasted_iota(I32, (E, 1), 0)
    eidx = eidx_ref[...]
    ps_col = pstart[:, 0:1]
    for k in range(TOP_K):
        e = eidx[k:k + 1, :]
        base = jnp.sum(jnp.where(e == eio, ps_col, 0), axis=0, keepdims=True)
        dest_ref[k:k + 1, :] = jnp.where(e >= 0, base + rank_ref[k:k + 1, :], -1)

    @pl.when(pl.program_id(0) == 0)
    def _():
        blk0 = lax.broadcasted_iota(I32, (1, NBP), 1) * bm
        be = jnp.sum((pend[:, 0:1] <= blk0).astype(I32), axis=0, keepdims=True)
        be = jnp.minimum(be, E - 1)
        last = jnp.sum(jnp.where(eio == be, (pstart + cnt)[:, 0:1], 0), axis=0, keepdims=True)
        be_ref[...] = be
        nr_ref[...] = jnp.clip(last - blk0, 0, bm)


def moe_plan(eidx, cfg: Cfg, bm, tm):
    K, N = eidx.shape
    E = cfg.E
    NB = -(-(N * K) // bm) + E
    NBP = -(-NB // V7X_LANES) * V7X_LANES
    rank, cnt = pl.pallas_call(
        functools.partial(_rank_kernel, E=E),
        grid=(N // tm,),
        in_specs=[pl.BlockSpec((K, tm), lambda i: (0, i))],
        out_specs=[pl.BlockSpec((K, tm), lambda i: (0, i)), pl.BlockSpec((E, V7X_LANES), lambda i: (0, 0))],
        out_shape=[jax.ShapeDtypeStruct((K, N), I32), jax.ShapeDtypeStruct((E, V7X_LANES), I32)],
        scratch_shapes=[pltpu.VMEM((E, V7X_LANES), F32)],
        compiler_params=_cparams(("arbitrary",)),
        name="moe_rank",
    )(eidx)
    dest, be, nr = pl.pallas_call(
        functools.partial(_plan_kernel, E=E, bm=bm, NBP=NBP),
        grid=(N // tm,),
        in_specs=[pl.BlockSpec((K, tm), lambda i: (0, i)), pl.BlockSpec((K, tm), lambda i: (0, i)),
                  pl.BlockSpec((E, V7X_LANES), lambda i: (0, 0))],
        out_specs=[pl.BlockSpec((K, tm), lambda i: (0, i)), pl.BlockSpec((1, NBP), lambda i: (0, 0)),
                   pl.BlockSpec((1, NBP), lambda i: (0, 0))],
        out_shape=[jax.ShapeDtypeStruct((K, N), I32), jax.ShapeDtypeStruct((1, NBP), I32),
                   jax.ShapeDtypeStruct((1, NBP), I32)],
        compiler_params=_cparams(("arbitrary",)),
        name="moe_plan",
    )(eidx, rank, cnt)
    return dest, be.reshape(NBP), nr.reshape(NBP), NB


def _dispatch_kernel(dest_ref, hp_ref, hs_ref, xs_ref, sem, *, tm, tiles_p):
    i = pl.program_id(0)

    def scatter_rows(h_ref, rows):
        def row_copy(t, k):
            return pltpu.make_async_copy(h_ref.at[pl.ds(t, 1)], xs_ref.at[pl.ds(dest_ref[k, t], 1)], sem)

        def start(t, c):
            for k in range(TOP_K):
                row_copy(t, k).start()
            return c

        def wait(t, c):
            for k in range(TOP_K):
                row_copy(t, k).wait()
            return c

        lax.fori_loop(0, rows, start, 0, unroll=DMA_LOOP_UNROLL)
        lax.fori_loop(0, rows, wait, 0, unroll=DMA_LOOP_UNROLL)

    @pl.when(i < tiles_p)
    def _():
        scatter_rows(hp_ref, tm)

    @pl.when(i >= tiles_p)
    def _():
        scatter_rows(hs_ref, hs_ref.shape[0])


def moe_dispatch(h2_p, h2_s, dest, P, tm):
    D = h2_p.shape[1]
    NS = h2_s.shape[0]
    assert h2_p.shape[0] % tm == 0 and NS <= tm
    tiles_p = h2_p.shape[0] // tm
    assert dest.shape[1] == (tiles_p + 1) * tm
    return pl.pallas_call(
        functools.partial(_dispatch_kernel, tm=tm, tiles_p=tiles_p),
        grid=(tiles_p + 1,),
        in_specs=[pl.BlockSpec((TOP_K, tm), lambda i: (0, i), memory_space=pltpu.SMEM),
                  pl.BlockSpec((tm, D), lambda i: (jnp.minimum(i, tiles_p - 1), 0)),
                  pl.BlockSpec((NS, D), lambda i: (0, 0))],
        out_specs=pl.BlockSpec(memory_space=pl.ANY),
        out_shape=jax.ShapeDtypeStruct((P, D), F32),
        scratch_shapes=[pltpu.SemaphoreType.DMA(())],
        compiler_params=_cparams(("arbitrary",)),
        name="moe_dispatch",
    )(dest, h2_p, h2_s)


def _experts_kernel(be_ref, nr_ref, x_ref, wg_hbm, wu_hbm, wd_hbm, y_ref, wgf, wuf, wdf, wgb_ref, wub_ref, wdb_ref,
                    run_ref, sem):
    b = pl.program_id(0)
    nb = pl.num_programs(0)
    n = nr_ref[b]
    e = be_ref[b]

    def weight_copies(ee, slot):
        return (pltpu.make_async_copy(wg_hbm.at[ee], wgf.at[slot], sem.at[0, slot]),
                pltpu.make_async_copy(wu_hbm.at[ee], wuf.at[slot], sem.at[1, slot]),
                pltpu.make_async_copy(wd_hbm.at[ee], wdf.at[slot], sem.at[2, slot]))

    @pl.when(b == 0)
    def _():
        run_ref[0] = 0

        @pl.when(n > 0)
        def _():
            for cp in weight_copies(e, 0):
                cp.start()

    @pl.when((n > 0) & ((b == 0) | (e != be_ref[jnp.maximum(b - 1, 0)])))
    def _():
        slot = run_ref[0] % 2
        for cp in weight_copies(e, slot):
            cp.wait()
        nxt = lax.while_loop(lambda k: (k < nb) & (be_ref[jnp.minimum(k, nb - 1)] == e), lambda k: k + 1, b + 1)
        nxt_c = jnp.minimum(nxt, nb - 1)

        @pl.when((nxt < nb) & (nr_ref[nxt_c] > 0))
        def _():
            for cp in weight_copies(be_ref[nxt_c], 1 - slot):
                cp.start()

        wgb_ref[...] = wgf[slot].astype(BF16)
        wub_ref[...] = wuf[slot].astype(BF16)
        wdb_ref[...] = wdf[slot].astype(BF16)
        run_ref[0] = run_ref[0] + 1

    @pl.when(n > 0)
    def _():
        row = lax.broadcasted_iota(I32, (x_ref.shape[0], 1), 0)
        x = jnp.where(row < n, x_ref[...], 0.0).astype(BF16)
        gate = jnp.dot(x, wgb_ref[...], preferred_element_type=F32)
        up = jnp.dot(x, wub_ref[...], preferred_element_type=F32)
        y_ref[...] = jnp.dot((_silu(gate) * up).astype(BF16), wdb_ref[...], preferred_element_type=F32)

    @pl.when(n == 0)
    def _():
        y_ref[...] = jnp.zeros(y_ref.shape, F32)


def moe_experts(xs, be, nr, w_gate, w_up, w_down, NB, bm):
    P, D = xs.shape
    E, _, DE = w_gate.shape
    return pl.pallas_call(
        _experts_kernel,
        grid_spec=pltpu.PrefetchScalarGridSpec(
            num_scalar_prefetch=2,
            grid=(NB,),
            in_specs=[pl.BlockSpec((bm, D), lambda b, be, nr: (b, 0)),
                      pl.BlockSpec(memory_space=pl.ANY), pl.BlockSpec(memory_space=pl.ANY),
                      pl.BlockSpec(memory_space=pl.ANY)],
            out_specs=pl.BlockSpec((bm, D), lambda b, be, nr: (b, 0)),
            scratch_shapes=[pltpu.VMEM((2, D, DE), F32), pltpu.VMEM((2, D, DE), F32), pltpu.VMEM((2, DE, D), F32),
                            pltpu.VMEM((D, DE), BF16), pltpu.VMEM((D, DE), BF16), pltpu.VMEM((DE, D), BF16),
                            pltpu.SMEM((1,), I32), pltpu.SemaphoreType.DMA((3, 2))]),
        out_shape=jax.ShapeDtypeStruct((P, D), F32),
        compiler_params=_cparams(("arbitrary",)),
        name="moe_experts",
    )(be, nr, xs, w_gate, w_up, w_down)


def _combine_kernel(dest_ref, ys_ref, wts_ref, x_ref, ysh_ref, gt_ref, gf_ref, o_ref, ybuf_ref, sem, *, tm):
    def row_copy(t, k):
        return pltpu.make_async_copy(ys_ref.at[pl.ds(dest_ref[k, t], 1)], ybuf_ref.at[k, pl.ds(t, 1)], sem)

    def start(t, c):
        for k in range(TOP_K):
            row_copy(t, k).start()
        return c

    def wait(t, c):
        for k in range(TOP_K):
            row_copy(t, k).wait()
        return c

    lax.fori_loop(0, tm, start, 0, unroll=DMA_LOOP_UNROLL)
    lax.fori_loop(0, tm, wait, 0, unroll=DMA_LOOP_UNROLL)
    w = wts_ref[...]
    ffn = ysh_ref[...]
    for k in range(TOP_K):
        ffn = ffn + w[:, k:k + 1] * ybuf_ref[k]
    x2 = x_ref[...] + gt_ref[...] * ffn
    ms = jnp.mean(x2 * x2, axis=-1, keepdims=True)
    o_ref[...] = x2 * lax.rsqrt(ms + NORM_EPS) * gf_ref[...]


def moe_combine(ys, dest, wts_t, x1, ysh, gt, g_final, tm, rows_per_group):
    N, D = x1.shape
    G, R, _ = gt.shape
    tpg = rows_per_group // tm
    return pl.pallas_call(
        functools.partial(_combine_kernel, tm=tm),
        grid=(N // tm,),
        in_specs=[pl.BlockSpec((TOP_K, tm), lambda i: (0, i), memory_space=pltpu.SMEM),
                  pl.BlockSpec(memory_space=pl.ANY),
                  pl.BlockSpec((tm, TOP_K), lambda i: (i, 0)),
                  pl.BlockSpec((tm, D), lambda i: (i, 0)),
                  pl.BlockSpec((tm, D), lambda i: (i, 0)),
                  pl.BlockSpec((None, R, D), lambda i: (i // tpg, 0, 0)),
                  pl.BlockSpec((1, D), lambda i: (0, 0))],
        out_specs=pl.BlockSpec((tm, D), lambda i: (i, 0)),
        out_shape=jax.ShapeDtypeStruct((N, D), F32),
        scratch_shapes=[pltpu.VMEM((TOP_K, tm, D), F32), pltpu.SemaphoreType.DMA(())],
        compiler_params=_cparams(("arbitrary",)),
        name="moe_combine",
    )(dest, ys, wts_t, x1, ysh, gt, g_final.reshape(1, D))


def make_cfg(x_prompt, x_sample, cache_k, cache_kidx, state_gla, page_table, w_in, w_gla_a, w_router, w_exp_gate):
    B, T, D = x_prompt.shape
    DB = x_sample.shape[0]
    assert x_sample.shape[1] == 1 and w_in.shape[0] == 1
    _, _, PAGE, NKV, HD = cache_k.shape
    ID = cache_kidx.shape[-1]
    _, _, GH, DK, DV = state_gla.shape
    RANK = w_gla_a.shape[1]
    NH = D // HD
    rest = w_in.shape[2] - (2 * GH * DK + 2 * GH * DV + RANK + NH * HD + 2 * NKV * HD + ID + 2 * D)
    IH = rest // (ID + 1)
    assert IH * (ID + 1) == rest
    NPAGES = page_table.shape[1]
    return Cfg(D=D, B=B, T=T, DB=DB, GH=GH, DK=DK, DV=DV, RANK=RANK, NH=NH, NKV=NKV, HD=HD, IH=IH, ID=ID,
               E=w_router.shape[2], DE=w_exp_gate.shape[3], PAGE=PAGE, NPAGES=NPAGES, PAST=NPAGES * PAGE)


def _row_tile(n, cap):
    t = cap
    while n % t:
        t //= 2
    return t


def kernel(x_prompt, x_sample, c_prompt, c_sample, cache_k, cache_v, cache_kidx, state_gla, page_table, w_ada, b_ada,
           g_norm1, w_in, w_gla_a, b_gla_a, g_gla_norm, w_gla_o, w_attn_o, w_out, g_norm2, w_router, b_router,
           w_sh_gate, w_sh_up, w_sh_down, w_exp_gate, w_exp_up, w_exp_down, g_final):
    cfg = make_cfg(x_prompt, x_sample, cache_k, cache_kidx, state_gla, page_table, w_in, w_gla_a, w_router, w_exp_gate)
    lay = proj_layout(cfg)
    B, T, D, DB = cfg.B, cfg.T, cfg.D, cfg.DB
    NP_, NS = B * T, DB
    kvw = cfg.NKV * cfg.HD

    pad = (-(B + DB)) % V7X_SUBLANES
    c_all = jnp.concatenate([c_prompt, c_sample, jnp.zeros((pad, D), F32)], axis=0)
    mod = ada_mod(c_all, w_ada[0], b_ada[0])
    sh1p, sc1p, gt1p, sh2p, sc2p, gt2p = [m[:, None, :] for m in jnp.split(mod[:B], 6, axis=-1)]
    sh1s, sc1s, gt1s, sh2s, sc2s, gt2s = [m[None] for m in jnp.split(mod[B:B + DB], 6, axis=-1)]

    w_pad = w_in_views(w_in[0], cfg, lay)
    tm_p = _row_tile(T, 1024)
    xp = x_prompt.reshape(NP_, D)
    xs = x_sample.reshape(NS, D)
    proj_p = in_proj(xp, g_norm1[0], sc1p, sh1p, w_pad, rope_tables(np.arange(T), cfg), cfg, lay, tm_p, T)
    proj_s = in_proj(xs, g_norm1[0], sc1s, sh1s, w_pad, rope_tables(np.full((NS,), cfg.PAST), cfg), cfg, lay, NS, NS)

    def grp(proj, n, w=None):
        return proj[:, lay.offs[n]:lay.offs[n] + (lay.widths[n] if w is None else w)]

    proj_p3 = proj_p.reshape(B, T, lay.total)
    og_p, gla_p = gla_prompt(proj_p3, w_gla_a[0], b_gla_a[0], g_gla_norm[0], cfg, lay)
    og_s, gla_s = gla_sample(proj_s.reshape(DB, 1, lay.total), state_gla[0], w_gla_a[0], b_gla_a[0], g_gla_norm[0],
                             cfg, lay)

    k_p, v_p, kidx_p = grp(proj_p, "ak"), grp(proj_p, "av"), grp(proj_p, "ik", cfg.ID)
    k_s, v_s, kidx_s = grp(proj_s, "ak"), grp(proj_s, "av"), grp(proj_s, "ik", cfg.ID)
    oa_p = dsa_prompt(proj_p3, k_p.astype(BF16).reshape(B, T, kvw), v_p.astype(BF16).reshape(B, T, kvw), cfg, lay)
    oa_s = dsa_sample(grp(proj_s, "aq").reshape(DB, cfg.NH, cfg.HD), grp(proj_s, "iq").reshape(DB, cfg.IH, cfg.ID),
                      grp(proj_s, "iw", cfg.IH), kidx_s, k_s, v_s, cache_k, cache_v, cache_kidx, page_table, cfg)

    wgo, wao, wo = w_gla_o[0].astype(BF16), w_attn_o[0].astype(BF16), w_out[0].astype(BF16)
    mg_p = merge_branches(og_p.reshape(NP_, -1), oa_p.reshape(NP_, -1), wgo, wao, proj_p, cfg, lay, _row_tile(T, 512))
    mg_s = merge_branches(og_s.reshape(NS, -1), oa_s.reshape(NS, -1), wgo, wao, proj_s, cfg, lay, NS)
    x1_p = out_proj(mg_p, wo, xp, gt1p, _row_tile(T, 1024), T, lay.tn)
    x1_s = out_proj(mg_s, wo, xs, gt1s, NS, NS, lay.tn)

    tile = V7X_LANES
    assert NP_ % tile == 0 and NS <= tile
    n_tok = NP_ + tile
    wr_t = w_router[0].T
    tm_f = _row_tile(T, 256)
    h2_p, ysh_p, eidx_p, wts_p = ffn_pre(x1_p, g_norm2[0], sc2p, sh2p, wr_t, b_router[0], w_sh_gate[0], w_sh_up[0],
                                         w_sh_down[0], cfg, tm_f)
    h2_s, ysh_s, eidx_s, wts_s = ffn_pre(x1_s, g_norm2[0], sc2s, sh2s, wr_t, b_router[0], w_sh_gate[0], w_sh_up[0],
                                         w_sh_down[0], cfg, NS)
    eidx = jnp.concatenate([eidx_p, eidx_s, jnp.full((TOP_K, n_tok - NP_ - NS), -1, I32)], axis=1)
    bm = 256 if NP_ * TOP_K >= 256 * cfg.E * 4 else 128
    dest, be, nr, NB = moe_plan(eidx, cfg, bm, tile)
    xs_sorted = moe_dispatch(h2_p, h2_s, dest, NB * bm, tile)
    ys_sorted = moe_experts(xs_sorted, be, nr, w_exp_gate[0], w_exp_up[0], w_exp_down[0], NB, bm)
    tm_c = _row_tile(T, 128)
    y_p = moe_combine(ys_sorted, dest[:, :NP_], wts_p.T, x1_p, ysh_p, gt2p, g_final, tm_c, T)
    y_s = moe_combine(ys_sorted, dest[:, NP_:NP_ + NS], wts_s.T, x1_s, ysh_s, gt2s, g_final, NS, NS)

    return (y_p.reshape(B, T, D), y_s.reshape(DB, 1, D),
            k_p.reshape(1, B, T, cfg.NKV, cfg.HD), v_p.reshape(1, B, T, cfg.NKV, cfg.HD), kidx_p.reshape(1, B, T, cfg.ID),
            gla_p[None],
            k_s.reshape(1, DB, 1, cfg.NKV, cfg.HD), v_s.reshape(1, DB, 1, cfg.NKV, cfg.HD), kidx_s.reshape(1, DB, 1, cfg.ID),
            gla_s[None])
```

```python
import functools
import math
from typing import NamedTuple

import jax
import jax.numpy as jnp
import numpy as np
from jax import lax
from jax.experimental import pallas as pl
from jax.experimental.pallas import tpu as pltpu

F32 = jnp.float32
BF16 = jnp.bfloat16
I32 = jnp.int32

V7X_LANES = 128
V7X_SUBLANES = 8
BF16_ROW_TILE = 16
V7X_VMEM_LIMIT_BYTES = 56 * 1024 * 1024

GLA_TAU = 16.0
GLA_CHUNK = 64
ROPE_THETA = 500000.0
TOPK_MAX = 256
Q_BLOCK = 128
KEY_CHUNK = 1024
DMA_LOOP_UNROLL = 4
N_GROUPS = 8
TOPK_GROUPS = 4
TOP_K = 8
ROUTED_SCALE = 2.5
NORM_EPS = 1e-6
INT_MIN = -(2 ** 31)
INT_MAX = 2 ** 31 - 1
NEG_BIG = -1e30


class Cfg(NamedTuple):
    D: int
    B: int
    T: int
    DB: int
    GH: int
    DK: int
    DV: int
    RANK: int
    NH: int
    NKV: int
    HD: int
    IH: int
    ID: int
    E: int
    DE: int
    PAGE: int
    NPAGES: int
    PAST: int


def _cparams(sem, vmem=V7X_VMEM_LIMIT_BYTES):
    return pltpu.CompilerParams(dimension_semantics=sem, vmem_limit_bytes=vmem)


def _bdot(a, b):
    return jnp.dot(a.astype(BF16), b.astype(BF16), preferred_element_type=F32)


def _dot_nt(a, b):
    return lax.dot_general(a.astype(BF16), b.astype(BF16), (((1,), (1,)), ((), ())),
                           preferred_element_type=F32)


def _dot_tn(a, b):
    return lax.dot_general(a.astype(BF16), b.astype(BF16), (((0,), (0,)), ((), ())),
                           preferred_element_type=F32)


def _split2(x):
    hi = x.astype(BF16)
    lo = (x - hi.astype(F32)).astype(BF16)
    return hi, lo


def _split3(x):
    hi = x.astype(BF16)
    r1 = x - hi.astype(F32)
    mid = r1.astype(BF16)
    lo = (r1 - mid.astype(F32)).astype(BF16)
    return hi, mid, lo


def _sigmoid(x):
    return 1.0 / (1.0 + jnp.exp(-x))


def _silu(x):
    return x * _sigmoid(x)


def _ada_kernel(c_ref, w_ref, b_ref, o_ref):
    c = c_ref[...]
    o_ref[...] = _bdot(_silu(c), w_ref[...]) + b_ref[...]


def ada_mod(c, w_ada, b_ada):
    R, D = c.shape
    N = w_ada.shape[1]
    tn = 512 if N % 512 == 0 else 128
    return pl.pallas_call(
        _ada_kernel,
        grid=(N // tn,),
        in_specs=[pl.BlockSpec((R, D), lambda j: (0, 0)),
                  pl.BlockSpec((D, tn), lambda j: (0, j)),
                  pl.BlockSpec((1, tn), lambda j: (0, j))],
        out_specs=pl.BlockSpec((R, tn), lambda j: (0, j)),
        out_shape=jax.ShapeDtypeStruct((R, N), F32),
        compiler_params=_cparams(("arbitrary",)),
        name="ada_mod",
    )(c, w_ada, b_ada.reshape(1, N))


class ProjLayout(NamedTuple):
    names: tuple
    offs: dict
    widths: dict
    total: int
    tn: int
    chunk_types: tuple


def proj_layout(cfg: Cfg) -> ProjLayout:
    qk = cfg.GH * cfg.DK
    vv = cfg.GH * cfg.DV
    aq = cfg.NH * cfg.HD
    kv = cfg.NKV * cfg.HD
    iq = cfg.IH * cfg.ID
    big = [("gq", qk), ("gk", qk), ("gv", vv), ("gr", vv), ("aq", aq), ("ak", kv), ("av", kv),
           ("iq", iq), ("za", cfg.D), ("zb", cfg.D)]
    tn = 512
    for _, w in big:
        tn = math.gcd(tn, w)
    assert tn % V7X_LANES == 0
    offs, widths, types = {}, {}, []
    off = 0
    for n, w in big:
        offs[n], widths[n] = off, w
        t = {"aq": 1, "ak": 1, "iq": 2, "za": 3, "zb": 3}.get(n, 0)
        types += [t] * (w // V7X_LANES)
        off += w
    for n, t in (("ga", 0), ("ik", 2), ("iw", 0), ("pad", 0)):
        offs[n], widths[n] = off, V7X_LANES
        types.append(t)
        off += V7X_LANES
    assert off % tn == 0
    return ProjLayout(tuple(n for n, _ in big) + ("ga", "ik", "iw", "pad"), offs, widths, off, tn, tuple(types))


def w_in_views(w_in, cfg: Cfg, lay: ProjLayout):
    qk = cfg.GH * cfg.DK
    vv = cfg.GH * cfg.DV
    aq = cfg.NH * cfg.HD
    kv = cfg.NKV * cfg.HD
    iq = cfg.IH * cfg.ID
    src_w = dict(gq=qk, gk=qk, gv=vv, ga=cfg.RANK, gr=vv, aq=aq, ak=kv, av=kv, iq=iq, ik=cfg.ID, iw=cfg.IH,
                 za=cfg.D, zb=cfg.D)
    src_off, o = {}, 0
    for n in ("gq", "gk", "gv", "ga", "gr", "aq", "ak", "av", "iq", "ik", "iw", "za", "zb"):
        src_off[n] = o
        o += src_w[n]
    assert o == w_in.shape[1]
    D = w_in.shape[0]
    wt = jnp.swapaxes(w_in, 0, 1).astype(BF16)

    def rows(n):
        return wt[src_off[n]:src_off[n] + src_w[n]]

    def padded(a):
        return jnp.pad(a, ((0, V7X_LANES - a.shape[0]), (0, 0)))

    assert 2 * cfg.ID == V7X_LANES and cfg.RANK <= V7X_LANES and cfg.IH <= V7X_LANES
    small = jnp.concatenate([padded(rows("ga")), rows("ik"), rows("ik"), padded(rows("iw")),
                             jnp.zeros((V7X_LANES, D), BF16)], axis=0)
    tile_src = []
    for n in ("gq", "gk", "gv", "gr", "aq", "ak", "av", "iq", "za", "zb"):
        assert src_off[n] % BF16_ROW_TILE == 0 and lay.widths[n] == src_w[n]
        tile_src += [src_off[n] + t * lay.tn for t in range(src_w[n] // lay.tn)]
    return wt, small, np.asarray(tile_src, np.int32)


def rope_tables(pos, cfg: Cfg):
    pos = np.asarray(pos, np.float64)

    def one(width, rot):
        half = rot // 2
        inv = ROPE_THETA ** (-np.arange(half, dtype=np.float64) * 2.0 / rot)
        ang = pos[:, None] * inv[None, :]
        lane = np.arange(V7X_LANES) % width
        c = np.where(lane < rot, np.cos(ang)[:, lane % half], 1.0)
        s1 = np.where((lane >= half) & (lane < rot), np.sin(ang)[:, (lane - half) % half], 0.0)
        s2 = np.where(lane < half, -np.sin(ang)[:, lane % half], 0.0)
        return [c, s1, s2]

    tabs = one(cfg.HD, cfg.HD // 4) + one(cfg.ID, cfg.ID // 4)
    return jnp.asarray(np.concatenate(tabs, axis=1), F32)


def _inproj_kernel(src_ref, x_ref, g_ref, sc_ref, sh_ref, ws_ref, rope_ref, wt_hbm, o_ref, h_ref, wbuf, acc_ref, sem,
                   *, patterns, half_a, half_i, n_main):
    i = pl.program_id(0)
    j = pl.program_id(1)
    ni = pl.num_programs(0)
    tn = wbuf.shape[1]
    seq = i * n_main + j

    def tile_copy(jj, slot):
        return pltpu.make_async_copy(wt_hbm.at[pl.ds(pl.multiple_of(src_ref[jj], BF16_ROW_TILE), tn)],
                                     wbuf.at[slot], sem.at[slot])

    @pl.when((i == 0) & (j == 0))
    def _():
        tile_copy(0, 0).start()

    @pl.when(j == 0)
    def _():
        x = x_ref[...]
        ms = jnp.mean(x * x, axis=-1, keepdims=True)
        y = x * lax.rsqrt(ms + NORM_EPS) * g_ref[...]
        h_ref[...] = (y * (1.0 + sc_ref[...]) + sh_ref[...]).astype(BF16)

    dn = (((1,), (1,)), ((), ()))

    @pl.when(j < n_main)
    def _():
        slot = seq % 2
        tile_copy(j, slot).wait()

        @pl.when(j + 1 < n_main)
        def _():
            tile_copy(j + 1, 1 - slot).start()

        @pl.when((j + 1 == n_main) & (i + 1 < ni))
        def _():
            tile_copy(0, 1 - slot).start()

        acc_ref[...] = lax.dot_general(h_ref[...], wbuf[slot], dn, preferred_element_type=F32)

    @pl.when(j >= n_main)
    def _():
        acc_ref[...] = lax.dot_general(h_ref[...], ws_ref[...], dn, preferred_element_type=F32)

    acc = acc_ref[...]
    L = V7X_LANES

    def rot(a, t):
        base, half = (0, half_a) if t == 1 else (3 * L, half_i)
        c = rope_ref[:, base:base + L]
        s1 = rope_ref[:, base + L:base + 2 * L]
        s2 = rope_ref[:, base + 2 * L:base + 3 * L]
        return a * c + pltpu.roll(a, half, 1) * s1 + pltpu.roll(a, L - half, 1) * s2

    for pat, ranges in patterns:
        cond = None
        for lo, hi in ranges:
            c = (j >= lo) & (j < hi)
            cond = c if cond is None else (cond | c)

        @pl.when(cond)
        def _(pat=pat):
            if all(t == 0 for t in pat):
                o_ref[...] = acc
            elif all(t == 3 for t in pat):
                o_ref[...] = _sigmoid(acc)
            else:
                for ci, t in enumerate(pat):
                    a = acc[:, ci * L:(ci + 1) * L]
                    o_ref[:, ci * L:(ci + 1) * L] = a if t == 0 else rot(a, t)


def _tile_patterns(lay: ProjLayout):
    per = lay.tn // V7X_LANES
    ntiles = lay.total // lay.tn
    pats = {}
    for j in range(ntiles):
        pats.setdefault(tuple(lay.chunk_types[j * per:(j + 1) * per]), []).append(j)
    out = []
    for pat, js in pats.items():
        ranges, start, prev = [], js[0], js[0]
        for j in js[1:]:
            if j != prev + 1:
                ranges.append((start, prev + 1))
                start = j
            prev = j
        ranges.append((start, prev + 1))
        out.append((pat, tuple(ranges)))
    return tuple(out)


def in_proj(x, g, sc, sh, w_views, rope, cfg: Cfg, lay: ProjLayout, tm, rows_per_group):
    wt, w_small, tile_src = w_views
    N, D = x.shape
    G, R, _ = sc.shape
    tn = lay.tn
    tiles_per_group = rows_per_group // tm
    n_main = len(tile_src)
    assert N == G * rows_per_group and rows_per_group % tm == 0 and R in (1, tm)
    assert rope.shape[0] == rows_per_group and n_main * tn + w_small.shape[0] == lay.total
    kern = functools.partial(_inproj_kernel, patterns=_tile_patterns(lay), half_a=cfg.HD // 8, half_i=cfg.ID // 8,
                             n_main=n_main)
    mod_spec = pl.BlockSpec((None, R, D), lambda i, j, s: (i // tiles_per_group, 0, 0))
    return pl.pallas_call(
        kern,
        grid_spec=pltpu.PrefetchScalarGridSpec(
            num_scalar_prefetch=1,
            grid=(N // tm, lay.total // tn),
            in_specs=[pl.BlockSpec((tm, D), lambda i, j, s: (i, 0)),
                      pl.BlockSpec((1, D), lambda i, j, s: (0, 0)),
                      mod_spec, mod_spec,
                      pl.BlockSpec((tn, D), lambda i, j, s: (jnp.maximum(j - n_main, 0), 0)),
                      pl.BlockSpec((tm, 6 * V7X_LANES), lambda i, j, s: (i % tiles_per_group, 0)),
                      pl.BlockSpec(memory_space=pl.ANY)],
            out_specs=pl.BlockSpec((tm, tn), lambda i, j, s: (i, j)),
            scratch_shapes=[pltpu.VMEM((tm, D), BF16), pltpu.VMEM((2, tn, D), BF16), pltpu.VMEM((tm, tn), F32),
                            pltpu.SemaphoreType.DMA((2,))]),
        out_shape=jax.ShapeDtypeStruct((N, lay.total), F32),
        compiler_params=_cparams(("arbitrary", "arbitrary")),
        name="in_proj",
    )(jnp.asarray(tile_src), x, g.reshape(1, D), sc, sh, w_small, rope, wt)


def _log_sigmoid(z):
    return jnp.minimum(z, 0.0) - jnp.log1p(jnp.exp(-jnp.abs(z)))


def _gla_prompt_kernel(q_ref, k_ref, v_ref, r_ref, a_ref, wa_ref, ba_ref, gn_ref, o_ref, s_ref, st_ref,
                       *, C, nchunk, rank, scale):
    t = pl.program_id(2)

    @pl.when(t == 0)
    def _():
        st_ref[...] = jnp.zeros_like(st_ref)

    row = lax.broadcasted_iota(I32, (C, C), 0)
    col = lax.broadcasted_iota(I32, (C, C), 1)
    causal = row >= col
    tri = jnp.where(causal, 1.0, 0.0).astype(BF16)
    wa = wa_ref[...]
    ba = ba_ref[...]
    gn = gn_ref[...]
    for c in range(nchunk):
        sl = slice(c * C, (c + 1) * C)
        q = q_ref[0, sl, :]
        k = k_ref[0, sl, :]
        v = v_ref[0, sl, :]
        a = a_ref[0, sl, :][:, :rank]
        la = _log_sigmoid(_bdot(a, wa) + ba) * (1.0 / GLA_TAU)
        hi, mid, lo = _split3(la)
        bc = (jnp.dot(tri, hi, preferred_element_type=F32) + jnp.dot(tri, mid, preferred_element_type=F32)
              + jnp.dot(tri, lo, preferred_element_type=F32))
        bend = bc[C - 1:C, :]
        qd = q * scale * jnp.exp(bc)
        kd = k * jnp.exp(-bc)
        ke = k * jnp.exp(bend - bc)
        att = jnp.where(causal, _dot_nt(qd, kd), 0.0)
        st = st_ref[...]
        o = _bdot(att, v) + _dot_nt(qd, st)
        st_ref[...] = st * jnp.exp(bend) + _dot_tn(v, ke)
        ms = jnp.mean(o * o, axis=-1, keepdims=True)
        on = o * lax.rsqrt(ms + NORM_EPS) * gn
        o_ref[0, sl, :] = on * _silu(r_ref[0, sl, :])

    @pl.when(t == pl.num_programs(2) - 1)
    def _():
        s_ref[0, 0] = st_ref[...].T


def gla_prompt(proj, w_gla_a, b_gla_a, g_gla_norm, cfg: Cfg, lay: ProjLayout):
    B, T, _ = proj.shape
    C = min(GLA_CHUNK, T)
    assert T % C == 0
    tc = C
    while tc * 2 <= 512 and T % (tc * 2) == 0:
        tc *= 2
    DK, DV, GH = cfg.DK, cfg.DV, cfg.GH
    oq, ok, ov, orr, oa = (lay.offs[n] for n in ("gq", "gk", "gv", "gr", "ga"))
    assert oq % DK == 0 and ok % DK == 0 and ov % DV == 0 and orr % DV == 0 and oa % V7X_LANES == 0
    kern = functools.partial(_gla_prompt_kernel, C=C, nchunk=tc // C, rank=cfg.RANK, scale=DK ** -0.5)
    return pl.pallas_call(
        kern,
        grid=(B, GH, T // tc),
        in_specs=[pl.BlockSpec((1, tc, DK), lambda b, h, t: (b, t, oq // DK + h)),
                  pl.BlockSpec((1, tc, DK), lambda b, h, t: (b, t, ok // DK + h)),
                  pl.BlockSpec((1, tc, DV), lambda b, h, t: (b, t, ov // DV + h)),
                  pl.BlockSpec((1, tc, DV), lambda b, h, t: (b, t, orr // DV + h)),
                  pl.BlockSpec((1, tc, V7X_LANES), lambda b, h, t: (b, t, oa // V7X_LANES)),
                  pl.BlockSpec((cfg.RANK, DK), lambda b, h, t: (0, h)),
                  pl.BlockSpec((1, DK), lambda b, h, t: (0, h)),
                  pl.BlockSpec((1, DV), lambda b, h, t: (0, 0))],
        out_specs=[pl.BlockSpec((1, tc, DV), lambda b, h, t: (b, t, h)),
                   pl.BlockSpec((1, 1, DK, DV), lambda b, h, t: (b, h, 0, 0))],
        out_shape=[jax.ShapeDtypeStruct((B, T, GH * DV), F32),
                   jax.ShapeDtypeStruct((B, GH, DK, DV), F32)],
        scratch_shapes=[pltpu.VMEM((DV, DK), F32)],
        compiler_params=_cparams(("arbitrary", "arbitrary", "arbitrary")),
        name="gla_prompt",
    )(proj, proj, proj, proj, proj, w_gla_a, b_gla_a.reshape(1, -1), g_gla_norm.reshape(1, DV))


def _gla_sample_kernel(q_ref, k_ref, v_ref, r_ref, a_ref, wa_ref, ba_ref, gn_ref, s0_ref, o_ref, s_ref,
                       *, GH, DK, DV, rank, scale):
    a = jnp.broadcast_to(a_ref[0][:, :rank], (V7X_SUBLANES, rank))
    ones = jnp.ones((V7X_SUBLANES, V7X_LANES), BF16)
    first = lax.broadcasted_iota(I32, (V7X_SUBLANES, 1), 0) == 0
    for h in range(GH):
        q = q_ref[0][:, h * DK:(h + 1) * DK]
        k = k_ref[0][:, h * DK:(h + 1) * DK]
        v = v_ref[0][:, h * DV:(h + 1) * DV]
        la = _log_sigmoid(_bdot(a, wa_ref[:, h * DK:(h + 1) * DK]) + ba_ref[:, h * DK:(h + 1) * DK])
        la = la[0:1, :] * (1.0 / GLA_TAU)
        dec = jnp.exp(la)
        qd = q * scale * dec
        kd = k * jnp.exp(-la)
        att = jnp.sum(qd * kd, axis=-1, keepdims=True)
        s0 = s0_ref[0, h]
        qd8 = jnp.broadcast_to(qd, (V7X_SUBLANES, DK))
        qh, ql = _split2(qd8)
        sh, sl = _split2(s0)
        qs = (jnp.dot(qh, sh, preferred_element_type=F32) + jnp.dot(qh, sl, preferred_element_type=F32)
              + jnp.dot(ql, sh, preferred_element_type=F32))[0:1, :]
        o = att * v + qs
        def col(x, width):
            x8 = jnp.where(first, jnp.broadcast_to(x, (V7X_SUBLANES, x.shape[1])), 0.0)
            parts = _split3(x8)
            rhs = ones if width == V7X_LANES else jnp.ones((V7X_SUBLANES, width), BF16)
            acc = None
            for p in parts:
                d = lax.dot_general(p, rhs, (((0,), (0,)), ((), ())), preferred_element_type=F32)
                acc = d if acc is None else acc + d
            return acc

        dec_col = col(dec, V7X_LANES)[:, 0:1]
        k8 = jnp.where(first, jnp.broadcast_to(k, (V7X_SUBLANES, DK)), 0.0)
        v8 = jnp.where(first, jnp.broadcast_to(v, (V7X_SUBLANES, DV)), 0.0)
        kh, kl = _split2(k8)
        vh, vl = _split2(v8)
        dn = (((0,), (0,)), ((), ()))
        outer = (lax.dot_general(kh, vh, dn, preferred_element_type=F32)
                 + lax.dot_general(kh, vl, dn, preferred_element_type=F32)
                 + lax.dot_general(kl, vh, dn, preferred_element_type=F32))
        s_ref[0, h] = s0 * dec_col + outer
        ms = jnp.mean(o * o, axis=-1, keepdims=True)
        on = o * lax.rsqrt(ms + NORM_EPS) * gn_ref[...]
        o_ref[0, :, h * DV:(h + 1) * DV] = on * _silu(r_ref[0][:, h * DV:(h + 1) * DV])


def gla_sample(proj, state, w_gla_a, b_gla_a, g_gla_norm, cfg: Cfg, lay: ProjLayout):
    DB = proj.shape[0]
    DK, DV, GH = cfg.DK, cfg.DV, cfg.GH
    qk, vv = GH * DK, GH * DV
    oq, ok, ov, orr, oa = (lay.offs[n] for n in ("gq", "gk", "gv", "gr", "ga"))
    assert oq % qk == 0 and ok % qk == 0 and ov % vv == 0 and orr % vv == 0
    kern = functools.partial(_gla_sample_kernel, GH=GH, DK=DK, DV=DV, rank=cfg.RANK, scale=DK ** -0.5)
    return pl.pallas_call(
        kern,
        grid=(DB,),
        in_specs=[pl.BlockSpec((1, 1, qk), lambda b: (b, 0, oq // qk)),
                  pl.BlockSpec((1, 1, qk), lambda b: (b, 0, ok // qk)),
                  pl.BlockSpec((1, 1, vv), lambda b: (b, 0, ov // vv)),
                  pl.BlockSpec((1, 1, vv), lambda b: (b, 0, orr // vv)),
                  pl.BlockSpec((1, 1, V7X_LANES), lambda b: (b, 0, oa // V7X_LANES)),
                  pl.BlockSpec((cfg.RANK, qk), lambda b: (0, 0)),
                  pl.BlockSpec((1, qk), lambda b: (0, 0)),
                  pl.BlockSpec((1, DV), lambda b: (0, 0)),
                  pl.BlockSpec((1, GH, DK, DV), lambda b: (b, 0, 0, 0))],
        out_specs=[pl.BlockSpec((1, 1, vv), lambda b: (b, 0, 0)),
                   pl.BlockSpec((1, GH, DK, DV), lambda b: (b, 0, 0, 0))],
        out_shape=[jax.ShapeDtypeStruct((DB, 1, vv), F32),
                   jax.ShapeDtypeStruct((DB, GH, DK, DV), F32)],
        compiler_params=_cparams(("arbitrary",)),
        name="gla_sample",
    )(proj, proj, proj, proj, proj, w_gla_a, b_gla_a.reshape(1, -1), g_gla_norm.reshape(1, DV), state)


def _offset_word_to_f32(t):
    key = t ^ INT_MIN
    return pltpu.bitcast(key ^ ((key >> 31) & INT_MAX), F32)


def _dsa_prompt_kernel(iq_ref, iw_ref, aq_ref, kidx_ref, kbf_ref, vbf_ref, o_ref,
                       sc_ref, lg_ref, q2_ref, wb_ref, *, IH, ID, NH, G, HD, topk, wscale, sm_scale, KEY_CHUNK):
    qb = pl.program_id(1)
    QB = Q_BLOCK
    L = V7X_LANES
    CH = KEY_CHUNK // L
    nch = qb // CH + 1
    lane = lax.broadcasted_iota(I32, (QB, L), 1)
    rowq = lax.broadcasted_iota(I32, (QB, 1), 0) + qb * QB
    first = lane < ID

    for h in range(IH):
        pair = iq_ref[0][:, (h // 2) * L:(h // 2 + 1) * L]
        qq = jnp.where((lane // ID) == (h % 2), pair, pltpu.roll(pair, ID, 1))
        hi, lo = _split2(qq)
        q2_ref[h * QB:(h + 1) * QB, 0:L] = jnp.where(first, hi, lo)
        q2_ref[h * QB:(h + 1) * QB, L:2 * L] = jnp.where(first, hi, jnp.zeros_like(hi))
        wb_ref[h * QB:(h + 1) * QB, :] = jnp.broadcast_to(iw_ref[0][:, h:h + 1] * wscale, (QB, 2 * L))

    lane2 = lax.broadcasted_iota(I32, (QB, 2 * L), 1)
    first_k = lax.broadcasted_iota(I32, (2 * L, L), 1) < ID

    def score_body(c, carry):
        for half in range(CH // 2):
            k0 = c * KEY_CHUNK + half * 2 * L
            cols = slice(half * 2 * L, (half + 1) * 2 * L)

            @pl.when(k0 < (qb + 1) * QB)
            def _():
                kblk = kidx_ref[0, pl.ds(pl.multiple_of(k0, 2 * L), 2 * L), :]
                khi, klo = _split2(kblk)
                k2 = jnp.concatenate([khi, jnp.where(first_k, klo, jnp.zeros_like(klo))], axis=1)
                s = lax.dot_general(q2_ref[...], k2, (((1,), (1,)), ((), ())),
                                    preferred_element_type=F32)
                s = jnp.maximum(s, 0.0) * wb_ref[...]
                sc = s[0:QB]
                for h in range(1, IH):
                    sc = sc + s[h * QB:(h + 1) * QB]
                sc_ref[c, :, cols] = jnp.where(k0 + lane2 <= rowq, sc, -jnp.inf)

            @pl.when(k0 >= (qb + 1) * QB)
            def _():
                sc_ref[c, :, cols] = jnp.full((QB, 2 * L), -jnp.inf, F32)
        return carry

    lax.fori_loop(0, nch, score_body, 0)

    def sweep(fn, init, combine):
        def body(c, acc):
            slab = sc_ref[c]
            for j in range(CH):
                acc = combine(acc, fn(slab[:, j * L:(j + 1) * L], c * KEY_CHUNK + j * L + lane))
            return acc
        return lax.fori_loop(0, nch, body, init)

    def count(pred):
        acc = sweep(lambda s, pos: jnp.where(pred(s, pos), 1, 0), jnp.zeros((QB, L), I32), jnp.add)
        return jnp.sum(acc, axis=1, keepdims=True)

    want = jnp.minimum(rowq + 1, topk)

    def bit_body(i, t):
        cand = t | (jnp.int32(1) << (31 - i))
        cf = _offset_word_to_f32(cand)
        cnt = count(lambda s, pos: s >= cf)
        return jnp.where(cnt >= want, cand, t)

    thr0 = _offset_word_to_f32(lax.fori_loop(0, 32, bit_body, jnp.zeros((QB, 1), I32)))
    n_ge = count(lambda s, pos: s >= thr0)
    thr = jnp.min(sweep(lambda s, pos: jnp.where(s >= thr0, s, jnp.inf), jnp.full((QB, L), jnp.inf, F32),
                        jnp.minimum), axis=1, keepdims=True)
    cut = jnp.full((QB, 1), INT_MAX, I32)

    def selected(s, pos, thr, cut):
        return (s > thr) | ((s == thr) & (pos <= cut))

    def drop_last(_, tc):
        thr, cut = tc
        over = count(lambda s, pos: selected(s, pos, thr, cut)) > want
        low = jnp.min(sweep(lambda s, pos: jnp.where(selected(s, pos, thr, cut), s, jnp.inf),
                            jnp.full((QB, L), jnp.inf, F32), jnp.minimum), axis=1, keepdims=True)
        last = jnp.max(sweep(lambda s, pos: jnp.where(selected(s, pos, thr, cut) & (s == low), pos, -1),
                             jnp.full((QB, L), -1, I32), jnp.maximum), axis=1, keepdims=True)
        return jnp.where(over, low, thr), jnp.where(over, last - 1, cut)

    thr, cut = lax.fori_loop(0, jnp.max(n_ge - want), drop_last, (thr, cut))

    def mask_body(c, carry):
        slab = sc_ref[c]
        pos = c * KEY_CHUNK + lax.broadcasted_iota(I32, slab.shape, 1)
        sc_ref[c] = jnp.where(selected(slab, pos, thr, cut), 0.0, -jnp.inf)
        return carry

    lax.fori_loop(0, nch, mask_body, 0)

    c2 = sm_scale * math.log2(math.e)
    for g in range(NH // G):
        qg = jnp.concatenate([aq_ref[0][:, (g * G + j) * HD:(g * G + j + 1) * HD] for j in range(G)],
                             axis=0).astype(BF16)

        def logit_body(c, mx):
            rows = pl.ds(pl.multiple_of(c * KEY_CHUNK, KEY_CHUNK), KEY_CHUNK)
            s = lax.dot_general(qg, kbf_ref[0, rows, g * HD:(g + 1) * HD], (((1,), (1,)), ((), ())),
                                preferred_element_type=F32) * c2
            mask = sc_ref[c]
            for j in range(G):
                sj = s[j * QB:(j + 1) * QB] + mask
                lg_ref[c, j * QB:(j + 1) * QB, :] = sj
                for b in range(CH):
                    mx[j] = jnp.maximum(mx[j], sj[:, b * L:(b + 1) * L])
            return mx

        mx = lax.fori_loop(0, nch, logit_body, [jnp.full((QB, L), NEG_BIG, F32) for _ in range(G)])
        m = jnp.concatenate([jnp.broadcast_to(jnp.max(x, axis=1, keepdims=True), (QB, L)) for x in mx],
                            axis=0)

        def value_body(c, carry):
            ls, acc = carry
            rows = pl.ds(pl.multiple_of(c * KEY_CHUNK, KEY_CHUNK), KEY_CHUNK)
            ps = [jnp.exp2(lg_ref[c, :, b * L:(b + 1) * L] - m) for b in range(CH)]
            for p in ps:
                ls = ls + p
            p = jnp.concatenate(ps, axis=1).astype(BF16)
            acc = acc + jnp.dot(p, vbf_ref[0, rows, g * HD:(g + 1) * HD], preferred_element_type=F32)
            return ls, acc

        ls, acc = lax.fori_loop(0, nch, value_body, (jnp.zeros((G * QB, L), F32), jnp.zeros((G * QB, HD), F32)))
        out = acc / jnp.sum(ls, axis=1, keepdims=True)
        for j in range(G):
            o_ref[0, :, (g * G + j) * HD:(g * G + j + 1) * HD] = out[j * QB:(j + 1) * QB]


def dsa_prompt(proj, kbf, vbf, cfg: Cfg, lay: ProjLayout):
    B, T, _ = proj.shape
    QB = Q_BLOCK
    chunk = min(KEY_CHUNK, T)
    assert T % chunk == 0 and chunk % (2 * V7X_LANES) == 0 and cfg.ID * 2 == V7X_LANES and cfg.IH % 2 == 0
    topk = min(TOPK_MAX, T // 4)
    iqw, aqw, kvw = cfg.IH * cfg.ID, cfg.NH * cfg.HD, cfg.NKV * cfg.HD
    oiq, oaq, oik, oiw = (lay.offs[n] for n in ("iq", "aq", "ik", "iw"))
    assert oiq % iqw == 0 and oaq % aqw == 0
    G = cfg.NH // cfg.NKV
    kern = functools.partial(_dsa_prompt_kernel, IH=cfg.IH, ID=cfg.ID, NH=cfg.NH, G=G, HD=cfg.HD, topk=topk,
                             wscale=(cfg.IH ** -0.5) * (cfg.ID ** -0.5), sm_scale=cfg.HD ** -0.5, KEY_CHUNK=chunk)
    return pl.pallas_call(
        kern,
        grid=(B, T // QB),
        in_specs=[pl.BlockSpec((1, QB, iqw), lambda b, q: (b, q, oiq // iqw)),
                  pl.BlockSpec((1, QB, V7X_LANES), lambda b, q: (b, q, oiw // V7X_LANES)),
                  pl.BlockSpec((1, QB, aqw), lambda b, q: (b, q, oaq // aqw)),
                  pl.BlockSpec((1, T, V7X_LANES), lambda b, q: (b, 0, oik // V7X_LANES)),
                  pl.BlockSpec((1, T, kvw), lambda b, q: (b, 0, 0)),
                  pl.BlockSpec((1, T, kvw), lambda b, q: (b, 0, 0))],
        out_specs=pl.BlockSpec((1, QB, aqw), lambda b, q: (b, q, 0)),
        out_shape=jax.ShapeDtypeStruct((B, T, aqw), F32),
        scratch_shapes=[pltpu.VMEM((T // chunk, QB, chunk), F32),
                        pltpu.VMEM((T // chunk, G * QB, chunk), F32),
                        pltpu.VMEM((cfg.IH * QB, 2 * V7X_LANES), BF16),
                        pltpu.VMEM((cfg.IH * QB, 2 * V7X_LANES), F32)],
        compiler_params=_cparams(("arbitrary", "arbitrary")),
        name="dsa_prompt",
    )(proj, proj, proj, proj, kbf, vbf)


def _dsa_sample_select_kernel(pt_ref, q_ref, w_ref, kn_ref, kidx_hbm, idx_ref, kbuf, sc_ref, rank_ref, sel_ref, sem,
                              *, NP, NPP, PG, topk, wscale):
    b = pl.program_id(0)
    nb = pl.num_programs(0)
    slot = b % 2
    L = V7X_LANES

    def page_copy(sl, p, page):
        return pltpu.make_async_copy(kidx_hbm.at[0, page], kbuf.at[sl, p], sem.at[sl])

    def issue(bb, sl):
        def body(p, c):
            page_copy(sl, p, pt_ref[bb, p]).start()
            return c
        lax.fori_loop(0, NP, body, 0, unroll=math.gcd(NP, 2 * DMA_LOOP_UNROLL))

    @pl.when(b == 0)
    def _():
        issue(0, 0)

    @pl.when(b + 1 < nb)
    def _():
        issue(b + 1, 1 - slot)

    def wait_body(p, c):
        page_copy(slot, p, 0).wait()
        return c

    lax.fori_loop(0, NP, wait_body, 0, unroll=math.gcd(NP, 2 * DMA_LOOP_UNROLL))

    q = q_ref[0]
    w = w_ref[0][:, 0:1] * wscale
    qh, ql = _split2(q)
    q3 = jnp.concatenate([qh, ql, qh], axis=1)
    dn = (((1,), (1,)), ((), ()))

    def group_body(gi, c):
        kk = kbuf[slot, pl.ds(gi * PG, PG)]
        kk = jnp.concatenate([kk[j] for j in range(PG)], axis=1)
        kh, kl = _split2(kk)
        s = jnp.dot(q3, jnp.concatenate([kh, kh, kl], axis=0), preferred_element_type=F32)
        sc = jnp.sum(jnp.maximum(s, 0.0) * w, axis=0, keepdims=True)
        for j in range(PG):
            sc_ref[pl.ds(gi * PG + j, 1), :] = sc[:, j * L:(j + 1) * L]
        return c

    lax.fori_loop(0, NP // PG, group_body, 0)
    sn = jnp.sum(q * kn_ref[0], axis=1, keepdims=True)
    scn = jnp.sum(jnp.maximum(sn, 0.0) * w, axis=0, keepdims=True)
    tail = (NPP - NP, L)
    first = (lax.broadcasted_iota(I32, tail, 0) == 0) & (lax.broadcasted_iota(I32, tail, 1) == 0)
    sc_ref[NP:NPP, :] = jnp.where(first, scn, -jnp.inf)

    sc = sc_ref[...]
    pos = lax.broadcasted_iota(I32, sc.shape, 0) * L + lax.broadcasted_iota(I32, sc.shape, 1)

    def bit_body(i, t):
        cand = t | (jnp.int32(1) << (31 - i))
        hit = jnp.where(sc >= _offset_word_to_f32(cand), 1, 0)
        cnt = jnp.sum(jnp.sum(hit, axis=0, keepdims=True), axis=1, keepdims=True)
        return jnp.where(cnt >= topk, cand, t)

    thr0 = _offset_word_to_f32(lax.fori_loop(0, 32, bit_body, jnp.zeros((1, L), I32)))
    n_ge = jnp.sum(jnp.where(sc >= thr0, 1, 0))
    thr = jnp.min(jnp.where(sc >= thr0, sc, jnp.inf))
    cut = jnp.int32(INT_MAX)

    def selected(thr, cut):
        return (sc > thr) | ((sc == thr) & (pos <= cut))

    def drop_last(_, tc):
        thr, cut = tc
        sel = selected(thr, cut)
        low = jnp.min(jnp.where(sel, sc, jnp.inf))
        last = jnp.max(jnp.where(sel & (sc == low), pos, -1))
        return low, last - 1

    thr, cut = lax.fori_loop(0, n_ge - topk, drop_last, (thr, cut))
    sel = selected(thr, cut)

    self_ = jnp.where(sel, 1.0, 0.0)
    li = lax.broadcasted_iota(I32, (L, L), 0)
    lj = lax.broadcasted_iota(I32, (L, L), 1)
    before_lane = jnp.where(li < lj, 1.0, 0.0).astype(BF16)
    ri = lax.broadcasted_iota(I32, (NPP, NPP), 0)
    rj = lax.broadcasted_iota(I32, (NPP, NPP), 1)
    before_row = jnp.where(rj < ri, 1.0, 0.0).astype(BF16)
    rowtot = jnp.broadcast_to(jnp.sum(self_, axis=1, keepdims=True), (NPP, L))
    rank_ref[...] = (jnp.dot(self_.astype(BF16), before_lane, preferred_element_type=F32)
                     + jnp.dot(before_row, rowtot.astype(BF16), preferred_element_type=F32))
    sel_ref[...] = self_

    slot_id = lax.broadcasted_iota(I32, (topk, 1), 0).astype(F32)
    lane1 = lax.broadcasted_iota(I32, (1, L), 1)

    def place(r, acc):
        hit = (rank_ref[pl.ds(r, 1), :] == slot_id) & (sel_ref[pl.ds(r, 1), :] > 0.0)
        return acc + jnp.where(hit, (r * L + lane1).astype(F32), 0.0)

    placed = lax.fori_loop(0, NPP, place, jnp.zeros((topk, L), F32)).astype(I32)
    ones = jnp.ones((V7X_SUBLANES, L), BF16)
    lane_bits = L.bit_length() - 1
    hi = (placed >> lane_bits).astype(F32).astype(BF16)
    lo = (placed & (L - 1)).astype(F32).astype(BF16)
    idx = (lax.dot_general(ones, hi, dn, preferred_element_type=F32) * float(L)
           + lax.dot_general(ones, lo, dn, preferred_element_type=F32))
    idx_ref[0] = idx.astype(I32)


def _dsa_sample_attend_kernel(idx_s, pt_s, q_ref, idx_ref, kn_ref, vn_ref, ck_hbm, cv_hbm, o_ref, kbuf, vbuf, sem,
                              *, NKV, G, topk, past, page_shift, sm_scale):
    b = pl.program_id(0)
    nb = pl.num_programs(0)
    slot = b % 2
    page_mask = (1 << page_shift) - 1

    def row_copies(sl, j, page, r):
        return (pltpu.make_async_copy(ck_hbm.at[0, page, r], kbuf.at[sl, j], sem.at[0, sl]),
                pltpu.make_async_copy(cv_hbm.at[0, page, r], vbuf.at[sl, j], sem.at[1, sl]))

    def issue(bb, sl):
        def body(j, c):
            i = jnp.minimum(idx_s[bb, j], past - 1)
            for cp in row_copies(sl, j, pt_s[bb, i >> page_shift], i & page_mask):
                cp.start()
            return c
        lax.fori_loop(0, topk, body, 0, unroll=2 * DMA_LOOP_UNROLL)

    @pl.when(b == 0)
    def _():
        issue(0, 0)

    @pl.when(b + 1 < nb)
    def _():
        issue(b + 1, 1 - slot)

    def wait_body(j, c):
        for cp in row_copies(slot, j, 0, 0):
            cp.wait()
        return c

    lax.fori_loop(0, topk, wait_body, 0, unroll=2 * DMA_LOOP_UNROLL)

    idx = idx_ref[0][0:1, :]
    cached = idx < past
    new_sel = jnp.max(jnp.where(idx == past, 1, 0), axis=1, keepdims=True) > 0
    for g in range(NKV):
        qg = q_ref[0][g * G:(g + 1) * G, :]
        kg = kbuf[slot, :, g, :]
        vg = vbuf[slot, :, g, :]
        s = jnp.where(cached, _dot_nt(qg, kg) * sm_scale, -jnp.inf)
        sn = jnp.sum(qg * kn_ref[0][g:g + 1, :], axis=1, keepdims=True) * sm_scale
        sn = jnp.where(new_sel, sn, -jnp.inf)
        m = jnp.maximum(jnp.max(s, axis=1, keepdims=True), sn)
        p = jnp.exp(s - m)
        pn = jnp.exp(sn - m)
        l = jnp.sum(p, axis=1, keepdims=True) + pn
        o_ref[0, g * G:(g + 1) * G, :] = (_bdot(p, vg) + pn * vn_ref[0][g:g + 1, :]) / l


def dsa_sample(q_att, q_idx, w_idx, kn_idx, k_new, v_new, cache_k, cache_v, cache_kidx, page_table, cfg: Cfg):
    DB, NP, PAGE = cfg.DB, cfg.NPAGES, cfg.PAGE
    assert PAGE == V7X_LANES and cache_k.shape[0] == 1
    total = cfg.PAST + 1
    topk = min(TOPK_MAX, total // 4)
    assert topk <= total and topk % V7X_LANES == 0
    NPP = -(-(NP + 1) // V7X_SUBLANES) * V7X_SUBLANES
    PG = math.gcd(NP, 8)
    w_b = jnp.broadcast_to(w_idx[:, :, None], (DB, cfg.IH, V7X_LANES))
    sel_kern = functools.partial(_dsa_sample_select_kernel, NP=NP, NPP=NPP, PG=PG, topk=topk,
                                 wscale=(cfg.IH ** -0.5) * (cfg.ID ** -0.5))
    idx = pl.pallas_call(
        sel_kern,
        grid_spec=pltpu.PrefetchScalarGridSpec(
            num_scalar_prefetch=1,
            grid=(DB,),
            in_specs=[pl.BlockSpec((1, cfg.IH, cfg.ID), lambda b, pt: (b, 0, 0)),
                      pl.BlockSpec((1, cfg.IH, V7X_LANES), lambda b, pt: (b, 0, 0)),
                      pl.BlockSpec((1, 1, cfg.ID), lambda b, pt: (b, 0, 0)),
                      pl.BlockSpec(memory_space=pl.ANY)],
            out_specs=pl.BlockSpec((1, V7X_SUBLANES, topk), lambda b, pt: (b, 0, 0)),
            scratch_shapes=[pltpu.VMEM((2, NP, cfg.ID, PAGE), F32),
                            pltpu.VMEM((NPP, V7X_LANES), F32),
                            pltpu.VMEM((NPP, V7X_LANES), F32),
                            pltpu.VMEM((NPP, V7X_LANES), F32),
                            pltpu.SemaphoreType.DMA((2,))]),
        out_shape=jax.ShapeDtypeStruct((DB, V7X_SUBLANES, topk), I32),
        compiler_params=_cparams(("arbitrary",)),
        name="dsa_sample_select",
    )(page_table, q_idx, w_b, kn_idx.reshape(DB, 1, cfg.ID), jnp.swapaxes(cache_kidx, 2, 3))

    G = cfg.NH // cfg.NKV
    att_kern = functools.partial(_dsa_sample_attend_kernel, NKV=cfg.NKV, G=G, topk=topk, past=cfg.PAST,
                                 page_shift=PAGE.bit_length() - 1, sm_scale=cfg.HD ** -0.5)
    return pl.pallas_call(
        att_kern,
        grid_spec=pltpu.PrefetchScalarGridSpec(
            num_scalar_prefetch=2,
            grid=(DB,),
            in_specs=[pl.BlockSpec((1, cfg.NH, cfg.HD), lambda b, ix, pt: (b, 0, 0)),
                      pl.BlockSpec((1, V7X_SUBLANES, topk), lambda b, ix, pt: (b, 0, 0)),
                      pl.BlockSpec((1, cfg.NKV, cfg.HD), lambda b, ix, pt: (b, 0, 0)),
                      pl.BlockSpec((1, cfg.NKV, cfg.HD), lambda b, ix, pt: (b, 0, 0)),
                      pl.BlockSpec(memory_space=pl.ANY),
                      pl.BlockSpec(memory_space=pl.ANY)],
            out_specs=pl.BlockSpec((1, cfg.NH, cfg.HD), lambda b, ix, pt: (b, 0, 0)),
            scratch_shapes=[pltpu.VMEM((2, topk, cfg.NKV, cfg.HD), F32),
                            pltpu.VMEM((2, topk, cfg.NKV, cfg.HD), F32),
                            pltpu.SemaphoreType.DMA((2, 2))]),
        out_shape=jax.ShapeDtypeStruct((DB, cfg.NH, cfg.HD), F32),
        compiler_params=_cparams(("arbitrary",)),
        name="dsa_sample_attend",
    )(idx[:, 0, :], page_table, q_att, idx, k_new.reshape(DB, cfg.NKV, cfg.HD), v_new.reshape(DB, cfg.NKV, cfg.HD),
      cache_k, cache_v)


def _merge_kernel(og_ref, oa_ref, wg_ref, wa_ref, za_ref, zb_ref, o_ref, ogb_ref, oab_ref):
    @pl.when(pl.program_id(1) == 0)
    def _():
        ogb_ref[...] = og_ref[...].astype(BF16)
        oab_ref[...] = oa_ref[...].astype(BF16)

    a = jnp.dot(ogb_ref[...], wg_ref[...].astype(BF16), preferred_element_type=F32)
    b = jnp.dot(oab_ref[...], wa_ref[...].astype(BF16), preferred_element_type=F32)
    o_ref[...] = za_ref[...] * a + zb_ref[...] * b


def merge_branches(og, oa, w_gla_o, w_attn_o, proj, cfg: Cfg, lay: ProjLayout, tm):
    N = og.shape[0]
    D, tn = cfg.D, lay.tn
    oza, ozb = lay.offs["za"] // tn, lay.offs["zb"] // tn
    return pl.pallas_call(
        _merge_kernel,
        grid=(N // tm, D // tn),
        in_specs=[pl.BlockSpec((tm, og.shape[1]), lambda i, j: (i, 0)),
                  pl.BlockSpec((tm, oa.shape[1]), lambda i, j: (i, 0)),
                  pl.BlockSpec((og.shape[1], tn), lambda i, j: (0, j)),
                  pl.BlockSpec((oa.shape[1], tn), lambda i, j: (0, j)),
                  pl.BlockSpec((tm, tn), lambda i, j: (i, oza + j)),
                  pl.BlockSpec((tm, tn), lambda i, j: (i, ozb + j))],
        out_specs=pl.BlockSpec((tm, tn), lambda i, j: (i, j)),
        out_shape=jax.ShapeDtypeStruct((N, D), F32),
        scratch_shapes=[pltpu.VMEM((tm, og.shape[1]), BF16), pltpu.VMEM((tm, oa.shape[1]), BF16)],
        compiler_params=_cparams(("arbitrary", "arbitrary")),
        name="merge_branches",
    )(og, oa, w_gla_o, w_attn_o, proj, proj)


def _outproj_kernel(m_ref, w_ref, x_ref, gt_ref, o_ref, mb_ref):
    @pl.when(pl.program_id(1) == 0)
    def _():
        mb_ref[...] = m_ref[...].astype(BF16)

    o_ref[...] = x_ref[...] + gt_ref[...] * jnp.dot(mb_ref[...], w_ref[...].astype(BF16), preferred_element_type=F32)


def out_proj(merged, w_out, x, gt, tm, rows_per_group, tn):
    N, D = x.shape
    G, R, _ = gt.shape
    tpg = rows_per_group // tm
    return pl.pallas_call(
        _outproj_kernel,
        grid=(N // tm, D // tn),
        in_specs=[pl.BlockSpec((tm, D), lambda i, j: (i, 0)),
                  pl.BlockSpec((D, tn), lambda i, j: (0, j)),
                  pl.BlockSpec((tm, tn), lambda i, j: (i, j)),
                  pl.BlockSpec((None, R, tn), lambda i, j: (i // tpg, 0, j))],
        out_specs=pl.BlockSpec((tm, tn), lambda i, j: (i, j)),
        out_shape=jax.ShapeDtypeStruct((N, D), F32),
        scratch_shapes=[pltpu.VMEM((tm, D), BF16)],
        compiler_params=_cparams(("arbitrary", "arbitrary")),
        name="out_proj",
    )(merged, w_out, x, gt)


def _ffn_pre_kernel(x_ref, g_ref, sc_ref, sh_ref, wrt_ref, br_ref, wg_ref, wu_ref, wd_ref,
                    h2_ref, ysh_ref, eidx_ref, wts_ref,
                    wrh_ref, wrl_ref, wgb_ref, wub_ref, wdb_ref, *, E, NG, first_tile):
    @pl.when(pl.program_id(0) == first_tile)
    def _():
        hi, lo = _split2(wrt_ref[...])
        wrh_ref[...] = hi
        wrl_ref[...] = lo
        wgb_ref[...] = wg_ref[...].astype(BF16)
        wub_ref[...] = wu_ref[...].astype(BF16)
        wdb_ref[...] = wd_ref[...].astype(BF16)

    x = x_ref[...]
    tm = x.shape[0]
    ms = jnp.mean(x * x, axis=-1, keepdims=True)
    h2 = x * lax.rsqrt(ms + NORM_EPS) * g_ref[...] * (1.0 + sc_ref[...]) + sh_ref[...]
    h2_ref[...] = h2
    hh, hl = _split2(h2)

    gate = jnp.dot(hh, wgb_ref[...], preferred_element_type=F32)
    up = jnp.dot(hh, wub_ref[...], preferred_element_type=F32)
    ysh_ref[...] = jnp.dot((_silu(gate) * up).astype(BF16), wdb_ref[...], preferred_element_type=F32)

    dn = (((1,), (1,)), ((), ()))
    logit = (lax.dot_general(wrh_ref[...], hh, dn, preferred_element_type=F32)
             + lax.dot_general(wrh_ref[...], hl, dn, preferred_element_type=F32)
             + lax.dot_general(wrl_ref[...], hh, dn, preferred_element_type=F32))
    s = _sigmoid(logit)
    PG = E // NG
    s3 = s.reshape(NG, PG, tm)
    sel3 = (s + br_ref[:, 0:1]).reshape(NG, PG, tm)
    ninf = -jnp.inf
    ipg = lax.broadcasted_iota(I32, (NG, PG, tm), 1)
    m1 = jnp.max(sel3, axis=1, keepdims=True)
    i1 = jnp.min(jnp.where(sel3 == m1, ipg, PG), axis=1, keepdims=True)
    m2 = jnp.max(jnp.where(ipg == i1, ninf, sel3), axis=1, keepdims=True)
    gs = (m1 + m2).reshape(NG, tm)
    ig = lax.broadcasted_iota(I32, (NG, tm), 0)
    gmask = jnp.zeros((NG, tm), jnp.bool_)
    for _ in range(TOPK_GROUPS):
        m = jnp.max(gs, axis=0, keepdims=True)
        i = jnp.min(jnp.where(gs == m, ig, NG), axis=0, keepdims=True)
        pick = ig == i
        gmask = gmask | pick
        gs = jnp.where(pick, ninf, gs)
    cand = jnp.where(gmask.reshape(NG, 1, tm), sel3, ninf)
    ie = lax.broadcasted_iota(I32, (NG, PG, tm), 0) * PG + ipg
    idxs, ws = [], []
    for _ in range(TOP_K):
        m = jnp.max(jnp.max(cand, axis=0, keepdims=True), axis=1, keepdims=True)
        hit = jnp.where(cand == m, ie, E)
        i = jnp.min(jnp.min(hit, axis=0, keepdims=True), axis=1, keepdims=True)
        pick = ie == i
        wsel = jnp.where(pick, s3, 0.0)
        ws.append(jnp.sum(jnp.sum(wsel, axis=0, keepdims=True), axis=1, keepdims=True).reshape(1, tm))
        idxs.append(i.reshape(1, tm))
        cand = jnp.where(pick, ninf, cand)
    tot = ws[0]
    for w in ws[1:]:
        tot = tot + w
    for k in range(TOP_K):
        eidx_ref[k:k + 1, :] = idxs[k]
        wts_ref[k:k + 1, :] = ws[k] / tot * ROUTED_SCALE


def ffn_pre(x1, g, sc, sh, w_router_t, b_router, w_sh_gate, w_sh_up, w_sh_down, cfg: Cfg, tm):
    N, D = x1.shape
    G, R, _ = sc.shape
    tpg = (N // G) // tm
    E, DE = cfg.E, cfg.DE
    assert E // N_GROUPS == V7X_SUBLANES
    mod_spec = pl.BlockSpec((None, R, D), lambda i: (i // tpg, 0, 0))
    const = lambda shape: pl.BlockSpec(shape, lambda i: (0,) * len(shape))
    return pl.pallas_call(
        functools.partial(_ffn_pre_kernel, E=E, NG=N_GROUPS, first_tile=0),
        grid=(N // tm,),
        in_specs=[pl.BlockSpec((tm, D), lambda i: (i, 0)), const((1, D)), mod_spec, mod_spec,
                  const((E, D)), const((E, V7X_LANES)), const((D, DE)), const((D, DE)), const((DE, D))],
        out_specs=[pl.BlockSpec((tm, D), lambda i: (i, 0)),
                   pl.BlockSpec((tm, D), lambda i: (i, 0)),
                   pl.BlockSpec((TOP_K, tm), lambda i: (0, i)),
                   pl.BlockSpec((TOP_K, tm), lambda i: (0, i))],
        out_shape=[jax.ShapeDtypeStruct((N, D), F32),
                   jax.ShapeDtypeStruct((N, D), F32),
                   jax.ShapeDtypeStruct((TOP_K, N), I32),
                   jax.ShapeDtypeStruct((TOP_K, N), F32)],
        scratch_shapes=[pltpu.VMEM((E, D), BF16), pltpu.VMEM((E, D), BF16),
                        pltpu.VMEM((D, DE), BF16), pltpu.VMEM((D, DE), BF16), pltpu.VMEM((DE, D), BF16)],
        compiler_params=_cparams(("arbitrary",)),
        name="ffn_pre",
    )(x1, g.reshape(1, D), sc, sh, w_router_t, jnp.broadcast_to(b_router[:, None], (E, V7X_LANES)),
      w_sh_gate, w_sh_up, w_sh_down)


def _rank_kernel(eidx_ref, rank_ref, cnt_ref, carry_ref, *, E):
    i = pl.program_id(0)

    @pl.when(i == 0)
    def _():
        carry_ref[...] = jnp.zeros(carry_ref.shape, F32)

    eidx = eidx_ref[...]
    tm = eidx.shape[1]
    eio = lax.broadcasted_iota(I32, (E, 1), 0)
    onehot = jnp.zeros((E, tm), F32)
    for k in range(TOP_K):
        onehot = onehot + jnp.where(eidx[k:k + 1, :] == eio, 1.0, 0.0)
    r = lax.broadcasted_iota(I32, (tm, tm), 0)
    c = lax.broadcasted_iota(I32, (tm, tm), 1)
    tri = jnp.where(r <= c, 1.0, 0.0).astype(BF16)
    cum = jnp.dot(onehot.astype(BF16), tri, preferred_element_type=F32)
    excl = cum - onehot + carry_ref[:, 0:1]
    for k in range(TOP_K):
        hit = eidx[k:k + 1, :] == eio
        rank_ref[k:k + 1, :] = jnp.sum(jnp.where(hit, excl, 0.0), axis=0, keepdims=True).astype(I32)
    carry_ref[...] = carry_ref[...] + cum[:, tm - 1:tm]

    @pl.when(i == pl.num_programs(0) - 1)
    def _():
        cnt_ref[...] = carry_ref[...].astype(I32)


def _plan_kernel(eidx_ref, rank_ref, cnt_ref, dest_ref, be_ref, nr_ref, *, E, bm, NBP):
    L = V7X_LANES
    sh = bm.bit_length() - 1
    cnt = cnt_ref[...]
    padded = ((cnt + (bm - 1)) >> sh) << sh
    r = lax.broadcasted_iota(I32, (E, E), 0)
    c = lax.broadcasted_iota(I32, (E, E), 1)
    low = jnp.where(c < r, 1.0, 0.0).astype(BF16)
    pstart = jnp.zeros((E, L), F32)
    for shift in (0, 8, 16):
        piece = ((padded >> shift) & 255).astype(F32).astype(BF16)
        pstart = pstart + jnp.dot(low, piece, preferred_element_type=F32) * float(1 << shift)
    pstart = pstart.astype(I32)
    pend = pstart + padded
    eio = lax.broadcasted_iota(I32, (E, 1), 0)
    eidx = eidx_ref[...]
    ps_col = pstart[:, 0:1]
    for k in range(TOP_K):
        e = eidx[k:k + 1, :]
        base = jnp.sum(jnp.where(e == eio, ps_col, 0), axis=0, keepdims=True)
        dest_ref[k:k + 1, :] = jnp.where(e >= 0, base + rank_ref[k:k + 1, :], -1)

    @pl.when(pl.program_id(0) == 0)
    def _():
        blk0 = lax.broadcasted_iota(I32, (1, NBP), 1) * bm
        be = jnp.sum((pend[:, 0:1] <= blk0).astype(I32), axis=0, keepdims=True)
        be = jnp.minimum(be, E - 1)
        last = jnp.sum(jnp.where(eio == be, (pstart + cnt)[:, 0:1], 0), axis=0, keepdims=True)
        be_ref[...] = be
        nr_ref[...] = jnp.clip(last - blk0, 0, bm)


def moe_plan(eidx, cfg: Cfg, bm, tm):
    K, N = eidx.shape
    E = cfg.E
    NB = -(-(N * K) // bm) + E
    NBP = -(-NB // V7X_LANES) * V7X_LANES
    rank, cnt = pl.pallas_call(
        functools.partial(_rank_kernel, E=E),
        grid=(N // tm,),
        in_specs=[pl.BlockSpec((K, tm), lambda i: (0, i))],
        out_specs=[pl.BlockSpec((K, tm), lambda i: (0, i)), pl.BlockSpec((E, V7X_LANES), lambda i: (0, 0))],
        out_shape=[jax.ShapeDtypeStruct((K, N), I32), jax.ShapeDtypeStruct((E, V7X_LANES), I32)],
        scratch_shapes=[pltpu.VMEM((E, V7X_LANES), F32)],
        compiler_params=_cparams(("arbitrary",)),
        name="moe_rank",
    )(eidx)
    dest, be, nr = pl.pallas_call(
        functools.partial(_plan_kernel, E=E, bm=bm, NBP=NBP),
        grid=(N // tm,),
        in_specs=[pl.BlockSpec((K, tm), lambda i: (0, i)), pl.BlockSpec((K, tm), lambda i: (0, i)),
                  pl.BlockSpec((E, V7X_LANES), lambda i: (0, 0))],
        out_specs=[pl.BlockSpec((K, tm), lambda i: (0, i)), pl.BlockSpec((1, NBP), lambda i: (0, 0)),
                   pl.BlockSpec((1, NBP), lambda i: (0, 0))],
        out_shape=[jax.ShapeDtypeStruct((K, N), I32), jax.ShapeDtypeStruct((1, NBP), I32),
                   jax.ShapeDtypeStruct((1, NBP), I32)],
        compiler_params=_cparams(("arbitrary",)),
        name="moe_plan",
    )(eidx, rank, cnt)
    return dest, be.reshape(NBP), nr.reshape(NBP), NB


def _dispatch_kernel(dest_ref, hp_ref, hs_ref, xs_ref, sem, *, tm, tiles_p):
    i = pl.program_id(0)

    def scatter_rows(h_ref, rows):
        def row_copy(t, k):
            return pltpu.make_async_copy(h_ref.at[pl.ds(t, 1)], xs_ref.at[pl.ds(dest_ref[k, t], 1)], sem)

        def start(t, c):
            for k in range(TOP_K):
                row_copy(t, k).start(priority=k % 2)
            return c

        def wait(t, c):
            for k in range(TOP_K):
                row_copy(t, k).wait()
            return c

        lax.fori_loop(0, rows, start, 0, unroll=DMA_LOOP_UNROLL)
        lax.fori_loop(0, rows, wait, 0, unroll=DMA_LOOP_UNROLL)

    @pl.when(i < tiles_p)
    def _():
        scatter_rows(hp_ref, tm)

    @pl.when(i >= tiles_p)
    def _():
        scatter_rows(hs_ref, hs_ref.shape[0])


def moe_dispatch(h2_p, h2_s, dest, P, tm):
    D = h2_p.shape[1]
    NS = h2_s.shape[0]
    assert h2_p.shape[0] % tm == 0 and NS <= tm
    tiles_p = h2_p.shape[0] // tm
    assert dest.shape[1] == (tiles_p + 1) * tm
    return pl.pallas_call(
        functools.partial(_dispatch_kernel, tm=tm, tiles_p=tiles_p),
        grid=(tiles_p + 1,),
        in_specs=[pl.BlockSpec((TOP_K, tm), lambda i: (0, i), memory_space=pltpu.SMEM),
                  pl.BlockSpec((tm, D), lambda i: (jnp.minimum(i, tiles_p - 1), 0)),
                  pl.BlockSpec((NS, D), lambda i: (0, 0))],
        out_specs=pl.BlockSpec(memory_space=pl.ANY),
        out_shape=jax.ShapeDtypeStruct((P, D), F32),
        scratch_shapes=[pltpu.SemaphoreType.DMA(())],
        compiler_params=_cparams(("arbitrary",)),
        name="moe_dispatch",
    )(dest, h2_p, h2_s)


def _experts_kernel(be_ref, nr_ref, x_ref, wg_hbm, wu_hbm, wd_hbm, y_ref, wgf, wuf, wdf, wgb_ref, wub_ref, wdb_ref,
                    run_ref, sem):
    b = pl.program_id(0)
    nb = pl.num_programs(0)
    n = nr_ref[b]
    e = be_ref[b]

    def weight_copies(ee, slot):
        return (pltpu.make_async_copy(wg_hbm.at[ee], wgf.at[slot], sem.at[0, slot]),
                pltpu.make_async_copy(wu_hbm.at[ee], wuf.at[slot], sem.at[1, slot]),
                pltpu.make_async_copy(wd_hbm.at[ee], wdf.at[slot], sem.at[2, slot]))

    @pl.when(b == 0)
    def _():
        run_ref[0] = 0

        @pl.when(n > 0)
        def _():
            for cp in weight_copies(e, 0):
                cp.start()

    @pl.when((n > 0) & ((b == 0) | (e != be_ref[jnp.maximum(b - 1, 0)])))
    def _():
        slot = run_ref[0] % 2
        for cp in weight_copies(e, slot):
            cp.wait()
        nxt = lax.while_loop(lambda k: (k < nb) & (be_ref[jnp.minimum(k, nb - 1)] == e), lambda k: k + 1, b + 1)
        nxt_c = jnp.minimum(nxt, nb - 1)

        @pl.when((nxt < nb) & (nr_ref[nxt_c] > 0))
        def _():
            for cp in weight_copies(be_ref[nxt_c], 1 - slot):
                cp.start()

        wgb_ref[...] = wgf[slot].astype(BF16)
        wub_ref[...] = wuf[slot].astype(BF16)
        wdb_ref[...] = wdf[slot].astype(BF16)
        run_ref[0] = run_ref[0] + 1

    @pl.when(n > 0)
    def _():
        row = lax.broadcasted_iota(I32, (x_ref.shape[0], 1), 0)
        x = jnp.where(row < n, x_ref[...], 0.0).astype(BF16)
        gate = jnp.dot(x, wgb_ref[...], preferred_element_type=F32)
        up = jnp.dot(x, wub_ref[...], preferred_element_type=F32)
        y_ref[...] = jnp.dot((_silu(gate) * up).astype(BF16), wdb_ref[...], preferred_element_type=F32)

    @pl.when(n == 0)
    def _():
        y_ref[...] = jnp.zeros(y_ref.shape, F32)


def moe_experts(xs, be, nr, w_gate, w_up, w_down, NB, bm):
    P, D = xs.shape
    E, _, DE = w_gate.shape
    return pl.pallas_call(
        _experts_kernel,
        grid_spec=pltpu.PrefetchScalarGridSpec(
            num_scalar_prefetch=2,
            grid=(NB,),
            in_specs=[pl.BlockSpec((bm, D), lambda b, be, nr: (b, 0)),
                      pl.BlockSpec(memory_space=pl.ANY), pl.BlockSpec(memory_space=pl.ANY),
                      pl.BlockSpec(memory_space=pl.ANY)],
            out_specs=pl.BlockSpec((bm, D), lambda b, be, nr: (b, 0)),
            scratch_shapes=[pltpu.VMEM((2, D, DE), F32), pltpu.VMEM((2, D, DE), F32), pltpu.VMEM((2, DE, D), F32),
                            pltpu.VMEM((D, DE), BF16), pltpu.VMEM((D, DE), BF16), pltpu.VMEM((DE, D), BF16),
                            pltpu.SMEM((1,), I32), pltpu.SemaphoreType.DMA((3, 2))]),
        out_shape=jax.ShapeDtypeStruct((P, D), F32),
        compiler_params=_cparams(("arbitrary",)),
        name="moe_experts",
    )(be, nr, xs, w_gate, w_up, w_down)


def _combine_kernel(dest_ref, ys_ref, wts_ref, x_ref, ysh_ref, gt_ref, gf_ref, o_ref, ybuf_ref, sem, *, tm):
    def row_copy(t, k):
        return pltpu.make_async_copy(ys_ref.at[pl.ds(dest_ref[k, t], 1)], ybuf_ref.at[k, pl.ds(t, 1)], sem)

    def start(t, c):
        for k in range(TOP_K):
            row_copy(t, k).start(priority=k % 2)
        return c

    def wait(t, c):
        for k in range(TOP_K):
            row_copy(t, k).wait()
        return c

    lax.fori_loop(0, tm, start, 0, unroll=DMA_LOOP_UNROLL)
    lax.fori_loop(0, tm, wait, 0, unroll=DMA_LOOP_UNROLL)
    w = wts_ref[...]
    ffn = ysh_ref[...]
    for k in range(TOP_K):
        ffn = ffn + w[:, k:k + 1] * ybuf_ref[k]
    x2 = x_ref[...] + gt_ref[...] * ffn
    ms = jnp.mean(x2 * x2, axis=-1, keepdims=True)
    o_ref[...] = x2 * lax.rsqrt(ms + NORM_EPS) * gf_ref[...]


def moe_combine(ys, dest, wts_t, x1, ysh, gt, g_final, tm, rows_per_group):
    N, D = x1.shape
    G, R, _ = gt.shape
    tpg = rows_per_group // tm
    return pl.pallas_call(
        functools.partial(_combine_kernel, tm=tm),
        grid=(N // tm,),
        in_specs=[pl.BlockSpec((TOP_K, tm), lambda i: (0, i), memory_space=pltpu.SMEM),
                  pl.BlockSpec(memory_space=pl.ANY),
                  pl.BlockSpec((tm, TOP_K), lambda i: (i, 0)),
                  pl.BlockSpec((tm, D), lambda i: (i, 0)),
                  pl.BlockSpec((tm, D), lambda i: (i, 0)),
                  pl.BlockSpec((None, R, D), lambda i: (i // tpg, 0, 0)),
                  pl.BlockSpec((1, D), lambda i: (0, 0))],
        out_specs=pl.BlockSpec((tm, D), lambda i: (i, 0)),
        out_shape=jax.ShapeDtypeStruct((N, D), F32),
        scratch_shapes=[pltpu.VMEM((TOP_K, tm, D), F32), pltpu.SemaphoreType.DMA(())],
        compiler_params=_cparams(("arbitrary",)),
        name="moe_combine",
    )(dest, ys, wts_t, x1, ysh, gt, g_final.reshape(1, D))


def make_cfg(x_prompt, x_sample, cache_k, cache_kidx, state_gla, page_table, w_in, w_gla_a, w_router, w_exp_gate):
    B, T, D = x_prompt.shape
    DB = x_sample.shape[0]
    assert x_sample.shape[1] == 1 and w_in.shape[0] == 1
    _, _, PAGE, NKV, HD = cache_k.shape
    ID = cache_kidx.shape[-1]
    _, _, GH, DK, DV = state_gla.shape
    RANK = w_gla_a.shape[1]
    NH = D // HD
    rest = w_in.shape[2] - (2 * GH * DK + 2 * GH * DV + RANK + NH * HD + 2 * NKV * HD + ID + 2 * D)
    IH = rest // (ID + 1)
    assert IH * (ID + 1) == rest
    NPAGES = page_table.shape[1]
    return Cfg(D=D, B=B, T=T, DB=DB, GH=GH, DK=DK, DV=DV, RANK=RANK, NH=NH, NKV=NKV, HD=HD, IH=IH, ID=ID,
               E=w_router.shape[2], DE=w_exp_gate.shape[3], PAGE=PAGE, NPAGES=NPAGES, PAST=NPAGES * PAGE)


def _row_tile(n, cap):
    t = cap
    while n % t:
        t //= 2
    return t


def kernel(x_prompt, x_sample, c_prompt, c_sample, cache_k, cache_v, cache_kidx, state_gla, page_table, w_ada, b_ada,
           g_norm1, w_in, w_gla_a, b_gla_a, g_gla_norm, w_gla_o, w_attn_o, w_out, g_norm2, w_router, b_router,
           w_sh_gate, w_sh_up, w_sh_down, w_exp_gate, w_exp_up, w_exp_down, g_final):
    cfg = make_cfg(x_prompt, x_sample, cache_k, cache_kidx, state_gla, page_table, w_in, w_gla_a, w_router, w_exp_gate)
    lay = proj_layout(cfg)
    B, T, D, DB = cfg.B, cfg.T, cfg.D, cfg.DB
    NP_, NS = B * T, DB
    kvw = cfg.NKV * cfg.HD

    pad = (-(B + DB)) % V7X_SUBLANES
    c_all = jnp.concatenate([c_prompt, c_sample, jnp.zeros((pad, D), F32)], axis=0)
    mod = ada_mod(c_all, w_ada[0], b_ada[0])
    sh1p, sc1p, gt1p, sh2p, sc2p, gt2p = [m[:, None, :] for m in jnp.split(mod[:B], 6, axis=-1)]
    sh1s, sc1s, gt1s, sh2s, sc2s, gt2s = [m[None] for m in jnp.split(mod[B:B + DB], 6, axis=-1)]

    w_pad = w_in_views(w_in[0], cfg, lay)
    tm_p = _row_tile(T, 1024)
    xp = x_prompt.reshape(NP_, D)
    xs = x_sample.reshape(NS, D)
    proj_p = in_proj(xp, g_norm1[0], sc1p, sh1p, w_pad, rope_tables(np.arange(T), cfg), cfg, lay, tm_p, T)
    proj_s = in_proj(xs, g_norm1[0], sc1s, sh1s, w_pad, rope_tables(np.full((NS,), cfg.PAST), cfg), cfg, lay, NS, NS)

    def grp(proj, n, w=None):
        return proj[:, lay.offs[n]:lay.offs[n] + (lay.widths[n] if w is None else w)]

    proj_p3 = proj_p.reshape(B, T, lay.total)
    og_p, gla_p = gla_prompt(proj_p3, w_gla_a[0], b_gla_a[0], g_gla_norm[0], cfg, lay)
    og_s, gla_s = gla_sample(proj_s.reshape(DB, 1, lay.total), state_gla[0], w_gla_a[0], b_gla_a[0], g_gla_norm[0],
                             cfg, lay)

    k_p, v_p, kidx_p = grp(proj_p, "ak"), grp(proj_p, "av"), grp(proj_p, "ik", cfg.ID)
    k_s, v_s, kidx_s = grp(proj_s, "ak"), grp(proj_s, "av"), grp(proj_s, "ik", cfg.ID)
    oa_p = dsa_prompt(proj_p3, k_p.astype(BF16).reshape(B, T, kvw), v_p.astype(BF16).reshape(B, T, kvw), cfg, lay)
    oa_s = dsa_sample(grp(proj_s, "aq").reshape(DB, cfg.NH, cfg.HD), grp(proj_s, "iq").reshape(DB, cfg.IH, cfg.ID),
                      grp(proj_s, "iw", cfg.IH), kidx_s, k_s, v_s, cache_k, cache_v, cache_kidx, page_table, cfg)

    wgo, wao, wo = w_gla_o[0].astype(BF16), w_attn_o[0].astype(BF16), w_out[0].astype(BF16)
    mg_p = merge_branches(og_p.reshape(NP_, -1), oa_p.reshape(NP_, -1), wgo, wao, proj_p, cfg, lay, _row_tile(T, 512))
    mg_s = merge_branches(og_s.reshape(NS, -1), oa_s.reshape(NS, -1), wgo, wao, proj_s, cfg, lay, NS)
    x1_p = out_proj(mg_p, wo, xp, gt1p, _row_tile(T, 1024), T, lay.tn)
    x1_s = out_proj(mg_s, wo, xs, gt1s, NS, NS, lay.tn)

    tile = V7X_LANES
    assert NP_ % tile == 0 and NS <= tile
    n_tok = NP_ + tile
    wr_t = w_router[0].T
    tm_f = _row_tile(T, 256)
    h2_p, ysh_p, eidx_p, wts_p = ffn_pre(x1_p, g_norm2[0], sc2p, sh2p, wr_t, b_router[0], w_sh_gate[0], w_sh_up[0],
                                         w_sh_down[0], cfg, tm_f)
    h2_s, ysh_s, eidx_s, wts_s = ffn_pre(x1_s, g_norm2[0], sc2s, sh2s, wr_t, b_router[0], w_sh_gate[0], w_sh_up[0],
                                         w_sh_down[0], cfg, NS)
    eidx = jnp.concatenate([eidx_p, eidx_s, jnp.full((TOP_K, n_tok - NP_ - NS), -1, I32)], axis=1)
    bm = 256 if NP_ * TOP_K >= 256 * cfg.E * 4 else 128
    dest, be, nr, NB = moe_plan(eidx, cfg, bm, tile)
    xs_sorted = moe_dispatch(h2_p, h2_s, dest, NB * bm, tile)
    ys_sorted = moe_experts(xs_sorted, be, nr, w_exp_gate[0], w_exp_up[0], w_exp_down[0], NB, bm)
    tm_c = _row_tile(T, 128)
    y_p = moe_combine(ys_sorted, dest[:, :NP_], wts_p.T, x1_p, ysh_p, gt2p, g_final, tm_c, T)
    y_s = moe_combine(ys_sorted, dest[:, NP_:NP_ + NS], wts_s.T, x1_s, ysh_s, gt2s, g_final, NS, NS)

    return (y_p.reshape(B, T, D), y_s.reshape(DB, 1, D),
            k_p.reshape(1, B, T, cfg.NKV, cfg.HD), v_p.reshape(1, B, T, cfg.NKV, cfg.HD), kidx_p.reshape(1, B, T, cfg.ID),
            gla_p[None],
            k_s.reshape(1, DB, 1, cfg.NKV, cfg.HD), v_s.reshape(1, DB, 1, cfg.NKV, cfg.HD), kidx_s.reshape(1, DB, 1, cfg.ID),
            gla_s[None])
```
